```python
import math
import jax, jax.numpy as jnp
from jax import lax
import numpy as np

D_MODEL = 1024
BATCH = 4
SEQ = 4096
DEPTH = 2

N_META = 16
N_MIXERS = 2
POOL_WINDOWS = (2, 4, 8, 16)
N_POOL_GROUPS = len(POOL_WINDOWS)
POOL_GROUP = D_MODEL // N_POOL_GROUPS
DIFF_HEAD_DIM = 64
DIFF_HEADS = D_MODEL // (2 * DIFF_HEAD_DIM)
D_FF = 256 * ((8 * D_MODEL // 3 + 255) // 256)
Q_BLOCK = 128
EPS = 1e-6
N_POOL_LAYERS = (DEPTH + 1) // 2
N_ATTN_LAYERS = DEPTH // 2

kernel_name = 'hybrid_pool_diffattn_macaron'


def rmsnorm(x, g):
    xf = x.astype(jnp.float32)
    y = xf * lax.rsqrt(jnp.mean(xf * xf, axis=-1, keepdims=True) + EPS)
    return (y * g.astype(jnp.float32)).astype(x.dtype)


def swiglu(h, w1, w3, w2):
    return (jax.nn.silu(h @ w1) * (h @ w3)) @ w2


def lambda_init_fn(layer_idx):
    return 0.8 - 0.6 * math.exp(-0.3 * layer_idx)


def causal_multiscale_pool(h, w, b, scale):
    bsz, L, D = h.shape
    hf = h.astype(jnp.float32)
    cs = jnp.pad(jnp.cumsum(hf, axis=1), ((0, 0), (1, 0), (0, 0)))
    hg = hf.reshape(bsz, L, N_POOL_GROUPS, POOL_GROUP)
    csg = cs.reshape(bsz, L + 1, N_POOL_GROUPS, POOL_GROUP)
    t = jnp.arange(L)
    pooled = []
    for g, win in enumerate(POOL_WINDOWS):
        lo = jnp.maximum(t + 1 - win, 0)
        cnt = (t + 1 - lo).astype(jnp.float32)
        pooled.append((csg[:, 1:, g] - csg[:, lo, g]) / cnt[None, :, None])
    pooled = jnp.stack(pooled, axis=2)
    diff = (pooled - hg).astype(h.dtype)
    y = jnp.einsum('blgc,gcd->blgd', diff, w) + b
    return y.reshape(bsz, L, D) * scale


def diff_attention(h, w_qkv, w_o, lq1, lk1, lq2, lk2, subln_g, lambda_init):
    bsz, L, D = h.shape
    H, d = DIFF_HEADS, DIFF_HEAD_DIM
    q, k, v = jnp.split(h @ w_qkv, 3, axis=-1)
    q = q.reshape(bsz, L, H, 2, d)
    k = k.reshape(bsz, L, H, 2, d)
    v = v.reshape(bsz, L, H, 2 * d)
    lam = (jnp.exp(jnp.sum(lq1.astype(jnp.float32) * lk1.astype(jnp.float32)))
           - jnp.exp(jnp.sum(lq2.astype(jnp.float32) * lk2.astype(jnp.float32)))
           + lambda_init)
    n_blk = -(-L // Q_BLOCK)
    Lp = n_blk * Q_BLOCK
    qb = jnp.pad(q, ((0, 0), (0, Lp - L), (0, 0), (0, 0), (0, 0)))
    qb = qb.reshape(bsz, n_blk, Q_BLOCK, H, 2, d).transpose(1, 0, 2, 3, 4, 5)
    key_pos = jnp.arange(L)
    neg = jnp.finfo(jnp.float32).min

    def block(args):
        qi, i = args
        s = jnp.einsum('bqhcd,bkhcd->bhcqk', qi, k).astype(jnp.float32) * (d ** -0.5)
        qpos = i * Q_BLOCK + jnp.arange(Q_BLOCK)
        mask = key_pos[None, :] <= qpos[:, None]
        p = jax.nn.softmax(jnp.where(mask, s, neg), axis=-1)
        a = p[:, :, 0] - lam * p[:, :, 1]
        return jnp.einsum('bhqk,bkhe->bqhe', a.astype(v.dtype), v)

    o = lax.map(block, (qb, jnp.arange(n_blk)))
    o = o.transpose(1, 0, 2, 3, 4).reshape(bsz, Lp, H, 2 * d)[:, :L]
    o = rmsnorm(o, subln_g) * (1.0 - lambda_init)
    return o.reshape(bsz, L, H * 2 * d) @ w_o


def setup_inputs(seed: int = 0) -> dict:
    key = jax.random.key(seed)
    ks = jax.random.split(key, 20)
    f32 = jnp.float32
    nrm = lambda k, s, sc: jax.random.normal(k, s, f32) * sc
    gain = lambda k, s: 1.0 + 0.02 * jax.random.normal(k, s, f32)
    NP, NA = N_POOL_LAYERS, N_ATTN_LAYERS
    return {
        'x': jax.random.normal(ks[0], (BATCH, SEQ, D_MODEL), f32),
        'meta_tokens': nrm(ks[1], (N_META, D_MODEL), 1.0),
        'ffn_norm_pre': gain(ks[2], (DEPTH, 2, D_MODEL)),
        'ffn_norm_post': gain(ks[3], (DEPTH, 2, D_MODEL)),
        'ffn_w1': nrm(ks[4], (DEPTH, 2, D_MODEL, D_FF), D_MODEL ** -0.5),
        'ffn_w3': nrm(ks[5], (DEPTH, 2, D_MODEL, D_FF), D_MODEL ** -0.5),
        'ffn_w2': nrm(ks[6], (DEPTH, 2, D_FF, D_MODEL), D_FF ** -0.5),
        'mix_norm_pre': gain(ks[7], (DEPTH, D_MODEL)),
        'mix_norm_post': gain(ks[8], (DEPTH, D_MODEL)),
        'pool_w': nrm(ks[9], (NP, N_POOL_GROUPS, POOL_GROUP, POOL_GROUP), POOL_GROUP ** -0.5),
        'pool_b': nrm(ks[10], (NP, N_POOL_GROUPS, POOL_GROUP), 0.01),
        'pool_scale': 1.0 + 0.1 * jax.random.normal(ks[11], (NP, D_MODEL), f32),
        'attn_w_qkv': nrm(ks[12], (NA, D_MODEL, 3 * D_MODEL), D_MODEL ** -0.5),
        'attn_w_o': nrm(ks[13], (NA, D_MODEL, D_MODEL), D_MODEL ** -0.5),
        'attn_lambda_q1': nrm(ks[14], (NA, DIFF_HEAD_DIM), 0.1),
        'attn_lambda_k1': nrm(ks[15], (NA, DIFF_HEAD_DIM), 0.1),
        'attn_lambda_q2': nrm(ks[16], (NA, DIFF_HEAD_DIM), 0.1),
        'attn_lambda_k2': nrm(ks[17], (NA, DIFF_HEAD_DIM), 0.1),
        'attn_subln': gain(ks[18], (NA, 2 * DIFF_HEAD_DIM)),
    }


def reference(x, meta_tokens, ffn_norm_pre, ffn_norm_post, ffn_w1, ffn_w3, ffn_w2,
              mix_norm_pre, mix_norm_post, pool_w, pool_b, pool_scale,
              attn_w_qkv, attn_w_o, attn_lambda_q1, attn_lambda_k1, attn_lambda_q2,
              attn_lambda_k2, attn_subln):
    bsz = x.shape[0]
    meta = jnp.broadcast_to(meta_tokens[None].astype(x.dtype), (bsz, N_META, x.shape[-1]))
    h = jnp.concatenate([meta, x], axis=1)
    for i in range(DEPTH):
        f = swiglu(rmsnorm(h, ffn_norm_pre[i, 0]), ffn_w1[i, 0], ffn_w3[i, 0], ffn_w2[i, 0])
        h = h + 0.5 * rmsnorm(f, ffn_norm_post[i, 0])
        hn = rmsnorm(h, mix_norm_pre[i])
        j = i // N_MIXERS
        if i % N_MIXERS == 0:
            m = causal_multiscale_pool(hn, pool_w[j], pool_b[j], pool_scale[j])
        else:
            m = diff_attention(hn, attn_w_qkv[j], attn_w_o[j], attn_lambda_q1[j],
                               attn_lambda_k1[j], attn_lambda_q2[j], attn_lambda_k2[j],
                               attn_subln[j], lambda_init_fn(i))
        h = h + rmsnorm(m, mix_norm_post[i])
        f = swiglu(rmsnorm(h, ffn_norm_pre[i, 1]), ffn_w1[i, 1], ffn_w3[i, 1], ffn_w2[i, 1])
        h = h + 0.5 * rmsnorm(f, ffn_norm_post[i, 1])
    return h[:, N_META:]
```

```python
import functools
import math

import jax
import jax.numpy as jnp
from jax import lax
from jax.experimental import pallas as pl
from jax.experimental.pallas import tpu as pltpu

D_MODEL = 1024
N_META = 16
POOL_WINDOWS = (2, 4, 8, 16)
POOL_GROUP = D_MODEL // len(POOL_WINDOWS)
HEAD_DIM = 64
HEAD_WIDTH = 2 * HEAD_DIM
N_HEADS = D_MODEL // HEAD_WIDTH
EPS = 1e-6
HALO = 16
MASK_VALUE = float(jnp.finfo(jnp.float32).min)

V7X_VMEM_BYTES = 64 * 1024 * 1024
VMEM_LIMIT_BYTES = 56 * 1024 * 1024

F32 = jnp.float32
BF16 = jnp.bfloat16


def _lambda_init(layer_idx):
    return 0.8 - 0.6 * math.exp(-0.3 * layer_idx)


def _rmsnorm(x, g):
    return x * lax.rsqrt(jnp.mean(x * x, axis=-1, keepdims=True) + EPS) * g


def _resident(shape):
    return pl.BlockSpec(shape, lambda *_: (0,) * len(shape), pipeline_mode=pl.Buffered(1))


def _params(semantics):
    return pltpu.CompilerParams(dimension_semantics=semantics, vmem_limit_bytes=VMEM_LIMIT_BYTES)


def _ffn_kernel(h_ref, gpre_ref, w1_ref, w3_ref, w2_ref, gpost_ref, o_ref, acc_ref, *, f_chunks):
    h = h_ref[...]
    hn = _rmsnorm(h, gpre_ref[...]).astype(BF16)
    for idx, (lo, size) in enumerate(f_chunks):
        a = jnp.dot(hn, w1_ref[:, lo:lo + size], preferred_element_type=F32)
        b = jnp.dot(hn, w3_ref[:, lo:lo + size], preferred_element_type=F32)
        g = (jax.nn.silu(a) * b).astype(BF16)
        part = jnp.dot(g, w2_ref[lo:lo + size, :], preferred_element_type=F32)
        if idx == 0:
            acc_ref[...] = part
        else:
            acc_ref[...] += part
    o_ref[...] = h + 0.5 * _rmsnorm(acc_ref[...], gpost_ref[...])


def _ffn(h, gpre, w1, w3, w2, gpost, *, tm):
    n, d = h.shape
    f = w1.shape[1]
    f_chunks = tuple((lo, min(1024, f - lo)) for lo in range(0, f, 1024))
    row = pl.BlockSpec((tm, d), lambda i: (i, 0))
    return pl.pallas_call(
        functools.partial(_ffn_kernel, f_chunks=f_chunks),
        out_shape=jax.ShapeDtypeStruct((n, d), F32),
        grid=(n // tm,),
        in_specs=[row, _resident((1, d)), _resident((d, f)), _resident((d, f)), _resident((f, d)),
                  _resident((1, d))],
        out_specs=row,
        scratch_shapes=[pltpu.VMEM((tm, d), F32)],
        compiler_params=_params(("parallel",)),
        name="ffn",
    )(h, gpre, w1, w3, w2, gpost)


def _pool_kernel(h_ref, prev_ref, meta_ref, gpre_ref, w_ref, b_ref, scale_ref, gpost_ref, o_ref, *,
                 is_meta):
    rows = h_ref.shape[0]
    h = h_ref[...]
    gpre = gpre_ref[...]
    hn = _rmsnorm(h, gpre)
    if is_meta:
        halo_n = jnp.zeros((HALO, D_MODEL), F32)
    else:
        halo = jnp.where(pl.program_id(1) == 0, meta_ref[...], prev_ref[...])
        halo_n = _rmsnorm(halo, gpre)
    ext = jnp.concatenate([halo_n, hn], axis=0)
    ys = []
    for gi, win in enumerate(POOL_WINDOWS):
        cols = slice(gi * POOL_GROUP, (gi + 1) * POOL_GROUP)
        acc = ext[:, cols]
        step = 1
        while step < win:
            acc = acc + pltpu.roll(acc, step, axis=0)
            step *= 2
        wsum = acc[HALO:]
        if is_meta:
            pos = lax.broadcasted_iota(jnp.int32, (rows, POOL_GROUP), 0)
            pooled = wsum / jnp.minimum(pos + 1, win).astype(F32)
        else:
            pooled = wsum * (1.0 / win)
        diff = (pooled - hn[:, cols]).astype(BF16)
        y = jnp.dot(diff, w_ref[gi], preferred_element_type=F32) + b_ref[:, cols]
        ys.append(y * scale_ref[:, cols])
    m = jnp.concatenate(ys, axis=1)
    o_ref[...] = h + _rmsnorm(m, gpost_ref[...])


def _pool(h, h_meta, gpre, w, b, scale, gpost, *, seq, tm, is_meta):
    n, d = h.shape
    if is_meta:
        grid = (1, 1)
        row = pl.BlockSpec((n, d), lambda bi, i: (0, 0))
        prev = pl.BlockSpec((HALO, d), lambda bi, i: (0, 0))
    else:
        grid = (n // seq, seq // tm)
        row = pl.BlockSpec((tm, d), lambda bi, i: (bi * (seq // tm) + i, 0))
        prev = pl.BlockSpec(
            (HALO, d), lambda bi, i: (jnp.maximum(bi * (seq // HALO) + i * (tm // HALO) - 1, 0), 0))
    g = len(POOL_WINDOWS)
    return pl.pallas_call(
        functools.partial(_pool_kernel, is_meta=is_meta),
        out_shape=jax.ShapeDtypeStruct((n, d), F32),
        grid=grid,
        in_specs=[row, prev, _resident((N_META, d)), _resident((1, d)),
                  _resident((g, POOL_GROUP, POOL_GROUP)), _resident((1, d)), _resident((1, d)),
                  _resident((1, d))],
        out_specs=row,
        compiler_params=_params(("parallel", "parallel")),
        name="pool_meta" if is_meta else "pool",
    )(h, h, h_meta, gpre, w, b, scale, gpost)


def _qkv_kernel(h_ref, g_ref, w_ref, q_ref, k_ref, v_ref):
    hn = _rmsnorm(h_ref[...], g_ref[...]).astype(BF16)
    d = D_MODEL
    q = jnp.dot(hn, w_ref[:, 0:d], preferred_element_type=F32)
    q_ref[...] = (q * (HEAD_DIM ** -0.5)).astype(BF16)
    k_ref[...] = jnp.dot(hn, w_ref[:, d:2 * d], preferred_element_type=F32).astype(BF16)
    v_ref[...] = jnp.dot(hn, w_ref[:, 2 * d:3 * d], preferred_element_type=F32).astype(BF16)


def _qkv(h, g, w, *, tm):
    n, d = h.shape
    row = pl.BlockSpec((tm, d), lambda i: (i, 0))
    out = jax.ShapeDtypeStruct((n, d), BF16)
    return pl.pallas_call(
        _qkv_kernel,
        out_shape=(out, out, out),
        grid=(n // tm,),
        in_specs=[row, _resident((1, d)), _resident((d, 3 * d))],
        out_specs=(row, row, row),
        compiler_params=_params(("parallel",)),
        name="qkv",
    )(h, g, w)


def _attn_kernel(q_ref, k_ref, v_ref, km_ref, vm_ref, lq1_ref, lk1_ref, lq2_ref, lk2_ref, subln_ref,
                 o_ref, qq_ref, m_ref, l_ref, acc_ref, *, tq, tk, lambda_init):
    qi = pl.program_id(2)
    nt = (((1,), (1,)), ((), ()))

    q = q_ref[...]
    lane = lax.broadcasted_iota(jnp.int32, q.shape, 1)
    zero = jnp.zeros_like(q)
    qq_ref[0:tq, :] = jnp.where(lane < HEAD_DIM, q, zero)
    qq_ref[tq:2 * tq, :] = jnp.where(lane >= HEAD_DIM, q, zero)

    s = lax.dot_general(qq_ref[...], km_ref[...], nt, preferred_element_type=F32)
    m0 = jnp.max(s, axis=1, keepdims=True)
    p = jnp.exp(s - m0)
    m_ref[...] = m0
    l_ref[...] = jnp.sum(p, axis=1, keepdims=True)
    acc_ref[...] = jnp.dot(p.astype(BF16), vm_ref[...], preferred_element_type=F32)

    def update(kj, masked):
        start = pl.multiple_of(kj * tk, tk)
        k = k_ref[pl.ds(start, tk), :]
        v = v_ref[pl.ds(start, tk), :]
        s = lax.dot_general(qq_ref[...], k, nt, preferred_element_type=F32)
        if masked:
            row = lax.broadcasted_iota(jnp.int32, (tq, tk), 0)
            col = lax.broadcasted_iota(jnp.int32, (tq, tk), 1)
            keep = jnp.concatenate([col <= row, col <= row], axis=0)
            s = jnp.where(keep, s, MASK_VALUE)
        m_prev = m_ref[...]
        m_new = jnp.maximum(m_prev, jnp.max(s, axis=1, keepdims=True))
        alpha = jnp.exp(m_prev - m_new)
        p = jnp.exp(s - m_new)
        l_ref[...] = alpha * l_ref[...] + jnp.sum(p, axis=1, keepdims=True)
        acc_ref[...] = alpha * acc_ref[...] + jnp.dot(p.astype(BF16), v, preferred_element_type=F32)
        m_ref[...] = m_new

    def body(kj, carry):
        update(kj, masked=False)
        return carry

    lax.fori_loop(0, qi, body, 0)
    update(qi, masked=True)

    lam = (jnp.exp(jnp.sum(lq1_ref[...] * lk1_ref[...], axis=1, keepdims=True))
           - jnp.exp(jnp.sum(lq2_ref[...] * lk2_ref[...], axis=1, keepdims=True)) + lambda_init)
    o = acc_ref[...] / l_ref[...]
    o = o[0:tq] - lam * o[tq:2 * tq]
    o_ref[...] = (_rmsnorm(o, subln_ref[...]) * (1.0 - lambda_init)).astype(o_ref.dtype)


def _attention(q, k, v, k_meta, v_meta, lq1, lk1, lq2, lk2, subln, *, seq, tq, lambda_init):
    n, d = q.shape
    nq = seq // tq
    hw = HEAD_WIDTH
    lam_spec = _resident((1, HEAD_DIM))
    return pl.pallas_call(
        functools.partial(_attn_kernel, tq=tq, tk=tq, lambda_init=lambda_init),
        out_shape=jax.ShapeDtypeStruct((n, d), BF16),
        grid=(n // seq, N_HEADS, nq),
        in_specs=[
            pl.BlockSpec((tq, hw), lambda b, h, i: (b * nq + i, h)),
            pl.BlockSpec((seq, hw), lambda b, h, i: (b, h)),
            pl.BlockSpec((seq, hw), lambda b, h, i: (b, h)),
            pl.BlockSpec((N_META, hw), lambda b, h, i: (0, h)),
            pl.BlockSpec((N_META, hw), lambda b, h, i: (0, h)),
            lam_spec, lam_spec, lam_spec, lam_spec, _resident((1, hw)),
        ],
        out_specs=pl.BlockSpec((tq, hw), lambda b, h, i: (b * nq + i, h)),
        scratch_shapes=[pltpu.VMEM((2 * tq, hw), BF16), pltpu.VMEM((2 * tq, 1), F32),
                        pltpu.VMEM((2 * tq, 1), F32), pltpu.VMEM((2 * tq, hw), F32)],
        compiler_params=_params(("parallel", "parallel", "parallel")),
        name="diff_attn",
    )(q, k, v, k_meta, v_meta, lq1, lk1, lq2, lk2, subln)


def _oproj_kernel(o_ref, h_ref, w_ref, g_ref, out_ref):
    m = jnp.dot(o_ref[...], w_ref[...], preferred_element_type=F32)
    out_ref[...] = h_ref[...] + _rmsnorm(m, g_ref[...])


def _oproj(o, h, w, g, *, tm):
    n, d = h.shape
    row = pl.BlockSpec((tm, d), lambda i: (i, 0))
    return pl.pallas_call(
        _oproj_kernel,
        out_shape=jax.ShapeDtypeStruct((n, d), F32),
        grid=(n // tm,),
        in_specs=[row, row, _resident((d, d)), _resident((1, d))],
        out_specs=row,
        compiler_params=_params(("parallel",)),
        name="oproj",
    )(o, h, w, g)


def kernel(x, meta_tokens, ffn_norm_pre, ffn_norm_post, ffn_w1, ffn_w3, ffn_w2, mix_norm_pre,
           mix_norm_post, pool_w, pool_b, pool_scale, attn_w_qkv, attn_w_o, attn_lambda_q1,
           attn_lambda_k1, attn_lambda_q2, attn_lambda_k2, attn_subln):
    bsz, seq, d = x.shape
    depth = ffn_w1.shape[0]
    assert d == D_MODEL and meta_tokens.shape == (N_META, d) and depth == 2
    tm = 512
    row = lambda a: a.reshape(1, -1)

    h = x.reshape(bsz * seq, d)
    hm = meta_tokens.astype(x.dtype)

    def ffn_pair(h, hm, i, s, need_meta=True):
        args = (row(ffn_norm_pre[i, s]), ffn_w1[i, s].astype(BF16), ffn_w3[i, s].astype(BF16),
                ffn_w2[i, s].astype(BF16), row(ffn_norm_post[i, s]))
        hm = _ffn(hm, *args, tm=N_META) if need_meta else None
        return _ffn(h, *args, tm=tm), hm

    h, hm = ffn_pair(h, hm, 0, 0)
    pool_args = (row(mix_norm_pre[0]), pool_w[0].astype(BF16), row(pool_b[0]), row(pool_scale[0]),
                 row(mix_norm_post[0]))
    h_new = _pool(h, hm, *pool_args, seq=seq, tm=tm, is_meta=False)
    hm = _pool(hm, hm, *pool_args, seq=N_META, tm=N_META, is_meta=True)
    h, hm = ffn_pair(h_new, hm, 0, 1)

    h, hm = ffn_pair(h, hm, 1, 0)
    g_pre = row(mix_norm_pre[1])
    w_qkv = attn_w_qkv[0].astype(BF16)
    q, k, v = _qkv(h, g_pre, w_qkv, tm=tm)
    _, k_meta, v_meta = _qkv(hm, g_pre, w_qkv, tm=N_META)
    o = _attention(q, k, v, k_meta, v_meta, row(attn_lambda_q1[0]), row(attn_lambda_k1[0]),
                   row(attn_lambda_q2[0]), row(attn_lambda_k2[0]), row(attn_subln[0]),
                   seq=seq, tq=tm, lambda_init=_lambda_init(1))
    h = _oproj(o, h, attn_w_o[0].astype(BF16), row(mix_norm_post[1]), tm=tm)
    h, _ = ffn_pair(h, None, 1, 1, need_meta=False)
    return h.reshape(bsz, seq, d)
```

```python
import functools
import math

import jax
import jax.numpy as jnp
from jax import lax
from jax.experimental import pallas as pl
from jax.experimental.pallas import tpu as pltpu

D_MODEL = 1024
N_META = 16
POOL_WINDOWS = (2, 4, 8, 16)
POOL_GROUP = D_MODEL // len(POOL_WINDOWS)
HEAD_DIM = 64
HEAD_WIDTH = 2 * HEAD_DIM
N_HEADS = D_MODEL // HEAD_WIDTH
EPS = 1e-6
HALO = 16
MASK_VALUE = float(jnp.finfo(jnp.float32).min)
LOG2_E = math.log2(math.e)

LANES = 128
ATTN_CHUNK = 256
VMEM_LIMIT_BYTES = 56 * 1024 * 1024

F32 = jnp.float32
BF16 = jnp.bfloat16


def _lambda_init(layer_idx):
    return 0.8 - 0.6 * math.exp(-0.3 * layer_idx)


def _rmsnorm(x, g):
    return x * lax.rsqrt(jnp.mean(x * x, axis=-1, keepdims=True) + EPS) * g


def _resident(shape):
    return pl.BlockSpec(shape, lambda *_: (0,) * len(shape), pipeline_mode=pl.Buffered(1))


def _params(semantics):
    return pltpu.CompilerParams(dimension_semantics=semantics, vmem_limit_bytes=VMEM_LIMIT_BYTES)


def _ffn_kernel(h_ref, gpre_ref, w1_ref, w3_ref, w2_ref, gpost_ref, o_ref, acc_ref, *, f_chunks):
    h = h_ref[...]
    hn = _rmsnorm(h, gpre_ref[...]).astype(BF16)
    for idx, (lo, size) in enumerate(f_chunks):
        a = jnp.dot(hn, w1_ref[:, lo:lo + size], preferred_element_type=F32)
        b = jnp.dot(hn, w3_ref[:, lo:lo + size], preferred_element_type=F32)
        g = (jax.nn.silu(a) * b).astype(BF16)
        part = jnp.dot(g, w2_ref[lo:lo + size, :], preferred_element_type=F32)
        if idx == 0:
            acc_ref[...] = part
        else:
            acc_ref[...] += part
    o_ref[...] = h + 0.5 * _rmsnorm(acc_ref[...], gpost_ref[...])


def _ffn(h, gpre, w1, w3, w2, gpost, *, tm):
    n, d = h.shape
    f = w1.shape[1]
    f_chunks = tuple((lo, min(1024, f - lo)) for lo in range(0, f, 1024))
    row = pl.BlockSpec((tm, d), lambda i: (i, 0))
    return pl.pallas_call(
        functools.partial(_ffn_kernel, f_chunks=f_chunks),
        out_shape=jax.ShapeDtypeStruct((n, d), F32),
        grid=(n // tm,),
        in_specs=[row, _resident((1, d)), _resident((d, f)), _resident((d, f)), _resident((f, d)),
                  _resident((1, d))],
        out_specs=row,
        scratch_shapes=[pltpu.VMEM((tm, d), F32)],
        compiler_params=_params(("parallel",)),
        name="ffn",
    )(h, gpre, w1, w3, w2, gpost)


def _pool_kernel(h_ref, prev_ref, meta_ref, gpre_ref, w_ref, b_ref, scale_ref, gpost_ref, o_ref, *,
                 is_meta):
    rows = h_ref.shape[0]
    h = h_ref[...]
    gpre = gpre_ref[...]
    hn = _rmsnorm(h, gpre)
    if is_meta:
        halo_n = jnp.zeros((HALO, D_MODEL), F32)
    else:
        halo = jnp.where(pl.program_id(1) == 0, meta_ref[...], prev_ref[...])
        halo_n = _rmsnorm(halo, gpre)
    ext = jnp.concatenate([halo_n, hn], axis=0)
    ys = []
    for gi, win in enumerate(POOL_WINDOWS):
        cols = slice(gi * POOL_GROUP, (gi + 1) * POOL_GROUP)
        acc = ext[:, cols]
        step = 1
        while step < win:
            acc = acc + pltpu.roll(acc, step, axis=0)
            step *= 2
        wsum = acc[HALO:]
        if is_meta:
            pos = lax.broadcasted_iota(jnp.int32, (rows, POOL_GROUP), 0)
            pooled = wsum / jnp.minimum(pos + 1, win).astype(F32)
        else:
            pooled = wsum * (1.0 / win)
        diff = (pooled - hn[:, cols]).astype(BF16)
        y = jnp.dot(diff, w_ref[gi], preferred_element_type=F32) + b_ref[:, cols]
        ys.append(y * scale_ref[:, cols])
    m = jnp.concatenate(ys, axis=1)
    o_ref[...] = h + _rmsnorm(m, gpost_ref[...])


def _pool(h, h_meta, gpre, w, b, scale, gpost, *, seq, tm, is_meta):
    n, d = h.shape
    if is_meta:
        grid = (1, 1)
        row = pl.BlockSpec((n, d), lambda bi, i: (0, 0))
        prev = pl.BlockSpec((HALO, d), lambda bi, i: (0, 0))
    else:
        grid = (n // seq, seq // tm)
        row = pl.BlockSpec((tm, d), lambda bi, i: (bi * (seq // tm) + i, 0))
        prev = pl.BlockSpec(
            (HALO, d), lambda bi, i: (jnp.maximum(bi * (seq // HALO) + i * (tm // HALO) - 1, 0), 0))
    g = len(POOL_WINDOWS)
    return pl.pallas_call(
        functools.partial(_pool_kernel, is_meta=is_meta),
        out_shape=jax.ShapeDtypeStruct((n, d), F32),
        grid=grid,
        in_specs=[row, prev, _resident((N_META, d)), _resident((1, d)),
                  _resident((g, POOL_GROUP, POOL_GROUP)), _resident((1, d)), _resident((1, d)),
                  _resident((1, d))],
        out_specs=row,
        compiler_params=_params(("parallel", "parallel")),
        name="pool_meta" if is_meta else "pool",
    )(h, h, h_meta, gpre, w, b, scale, gpost)


def _qkv_kernel(h_ref, g_ref, w_ref, q_ref, k_ref, v_ref):
    hn = _rmsnorm(h_ref[...], g_ref[...]).astype(BF16)
    d = D_MODEL
    q = jnp.dot(hn, w_ref[:, 0:d], preferred_element_type=F32)
    q_ref[...] = (q * (HEAD_DIM ** -0.5 * LOG2_E)).astype(BF16)
    k_ref[...] = jnp.dot(hn, w_ref[:, d:2 * d], preferred_element_type=F32).astype(BF16)
    v_ref[...] = jnp.dot(hn, w_ref[:, 2 * d:3 * d], preferred_element_type=F32).astype(BF16)


def _qkv(h, g, w, *, tm):
    n, d = h.shape
    row = pl.BlockSpec((tm, d), lambda i: (i, 0))
    out = jax.ShapeDtypeStruct((n, d), BF16)
    return pl.pallas_call(
        _qkv_kernel,
        out_shape=(out, out, out),
        grid=(n // tm,),
        in_specs=[row, _resident((1, d)), _resident((d, 3 * d))],
        out_specs=(row, row, row),
        compiler_params=_params(("parallel",)),
        name="qkv",
    )(h, g, w)


def _attn_kernel(q_ref, k_ref, v_ref, km_ref, vm_ref, lq1_ref, lk1_ref, lq2_ref, lk2_ref, subln_ref,
                 o_ref, qq_ref, vx_ref, kmx_ref, vmx_ref, m_ref, acc_ref, *, tq, tk, lambda_init):
    qi = pl.program_id(2)
    nt = (((1,), (1,)), ((), ()))
    ch, hw = ATTN_CHUNK, HEAD_WIDTH
    chunks_per_half = tq // ch

    @pl.when(qi == 0)
    def _stage_keys_values():
        vx_ref[:, 0:hw] = v_ref[...]
        vx_ref[:, hw:2 * hw] = jnp.ones((v_ref.shape[0], hw), BF16)
        kmx_ref[...] = jnp.zeros(kmx_ref.shape, BF16)
        kmx_ref[0:N_META, :] = km_ref[...]
        vmx_ref[...] = jnp.zeros(vmx_ref.shape, BF16)
        vmx_ref[0:N_META, 0:hw] = vm_ref[...]
        vmx_ref[0:N_META, hw:2 * hw] = jnp.ones((N_META, hw), BF16)

    q = q_ref[...]
    lane = lax.broadcasted_iota(jnp.int32, q.shape, 1)
    zero = jnp.zeros_like(q)
    qq_ref[0:tq, :] = jnp.where(lane < HEAD_DIM, q, zero)
    qq_ref[tq:2 * tq, :] = jnp.where(lane >= HEAD_DIM, q, zero)

    row_id = lax.broadcasted_iota(jnp.int32, (ch, LANES), 0)
    col_id = lax.broadcasted_iota(jnp.int32, (ch, LANES), 1)

    def chunk_update(c, segments, first=False):
        rows = slice(c * ch, (c + 1) * ch)
        qq = qq_ref[rows, :]
        seg_blocks = []
        for k, _, masks in segments:
            s = lax.dot_general(qq, k, nt, preferred_element_type=F32)
            blocks = []
            for j, keep in enumerate(masks):
                blk = s[:, j * LANES:(j + 1) * LANES]
                blocks.append(blk if keep is None else jnp.where(keep, blk, MASK_VALUE))
            seg_blocks.append(blocks)
        m_tile = functools.reduce(jnp.maximum, [blk for blocks in seg_blocks for blk in blocks])
        m_tile = jnp.max(m_tile, axis=1, keepdims=True)
        if first:
            m_new = jnp.broadcast_to(m_tile, (ch, LANES))
        else:
            m_prev = m_ref[rows, :]
            m_new = jnp.maximum(m_prev, m_tile)
            alpha = jnp.exp2(m_prev - m_new)
        pv = None
        for (_, vx, _), blocks in zip(segments, seg_blocks):
            p = jnp.concatenate([jnp.exp2(blk - m_new) for blk in blocks], axis=1).astype(BF16)
            part = jnp.dot(p, vx, preferred_element_type=F32)
            pv = part if pv is None else pv + part
        if first:
            acc_ref[rows, :] = pv
        else:
            acc_ref[rows, :] = jnp.concatenate([alpha, alpha], axis=1) * acc_ref[rows, :] + pv
        m_ref[rows, :] = m_new

    start = pl.multiple_of(qi * tq, tq)
    meta = (kmx_ref[...], vmx_ref[...], [col_id < N_META])
    for c in range(2 * chunks_per_half):
        n_keys = (c % chunks_per_half + 1) * ch
        masks = [None] * ((n_keys - ch) // LANES)
        masks += [col_id + j * LANES <= row_id for j in range(ch // LANES)]
        diag = (k_ref[pl.ds(start, n_keys), :], vx_ref[pl.ds(start, n_keys), :], masks)
        chunk_update(c, [meta, diag], first=True)

    def full_tile(kj, carry):
        start = pl.multiple_of(kj * tk, tk)
        seg = (k_ref[pl.ds(start, tk), :], vx_ref[pl.ds(start, tk), :], [None] * (tk // LANES))
        for c in range(2 * chunks_per_half):
            chunk_update(c, [seg])
        return carry

    lax.fori_loop(0, qi * (tq // tk), full_tile, 0)

    lam = (jnp.exp(jnp.sum(lq1_ref[...] * lk1_ref[...], axis=1, keepdims=True))
           - jnp.exp(jnp.sum(lq2_ref[...] * lk2_ref[...], axis=1, keepdims=True)) + lambda_init)
    acc = acc_ref[...]
    o = acc[:, 0:hw] / acc[:, hw:2 * hw]
    o = o[0:tq] - lam * o[tq:2 * tq]
    o_ref[...] = (_rmsnorm(o, subln_ref[...]) * (1.0 - lambda_init)).astype(o_ref.dtype)


def _attention(q, k, v, k_meta, v_meta, lq1, lk1, lq2, lk2, subln, *, seq, tq, tk, lambda_init):
    n, d = q.shape
    nq = seq // tq
    hw = HEAD_WIDTH
    lam_spec = _resident((1, HEAD_DIM))
    return pl.pallas_call(
        functools.partial(_attn_kernel, tq=tq, tk=tk, lambda_init=lambda_init),
        out_shape=jax.ShapeDtypeStruct((n, d), BF16),
        grid=(n // seq, N_HEADS, nq),
        in_specs=[
            pl.BlockSpec((tq, hw), lambda b, h, i: (b * nq + i, h)),
            pl.BlockSpec((seq, hw), lambda b, h, i: (b, h)),
            pl.BlockSpec((seq, hw), lambda b, h, i: (b, h)),
            pl.BlockSpec((N_META, hw), lambda b, h, i: (0, h)),
            pl.BlockSpec((N_META, hw), lambda b, h, i: (0, h)),
            lam_spec, lam_spec, lam_spec, lam_spec, _resident((1, hw)),
        ],
        out_specs=pl.BlockSpec((tq, hw), lambda b, h, i: (b * nq + i, h)),
        scratch_shapes=[
            pltpu.VMEM((2 * tq, hw), BF16),
            pltpu.VMEM((seq, 2 * hw), BF16),
            pltpu.VMEM((LANES, hw), BF16),
            pltpu.VMEM((LANES, 2 * hw), BF16),
            pltpu.VMEM((2 * tq, LANES), F32),
            pltpu.VMEM((2 * tq, 2 * hw), F32),
        ],
        compiler_params=_params(("parallel", "parallel", "arbitrary")),
        name="diff_attn",
    )(q, k, v, k_meta, v_meta, lq1, lk1, lq2, lk2, subln)


def _oproj_kernel(o_ref, h_ref, w_ref, g_ref, out_ref):
    m = jnp.dot(o_ref[...], w_ref[...], preferred_element_type=F32)
    out_ref[...] = h_ref[...] + _rmsnorm(m, g_ref[...])


def _oproj(o, h, w, g, *, tm):
    n, d = h.shape
    row = pl.BlockSpec((tm, d), lambda i: (i, 0))
    return pl.pallas_call(
        _oproj_kernel,
        out_shape=jax.ShapeDtypeStruct((n, d), F32),
        grid=(n // tm,),
        in_specs=[row, row, _resident((d, d)), _resident((1, d))],
        out_specs=row,
        compiler_params=_params(("parallel",)),
        name="oproj",
    )(o, h, w, g)


def kernel(x, meta_tokens, ffn_norm_pre, ffn_norm_post, ffn_w1, ffn_w3, ffn_w2, mix_norm_pre,
           mix_norm_post, pool_w, pool_b, pool_scale, attn_w_qkv, attn_w_o, attn_lambda_q1,
           attn_lambda_k1, attn_lambda_q2, attn_lambda_k2, attn_subln):
    bsz, seq, d = x.shape
    depth = ffn_w1.shape[0]
    assert d == D_MODEL and meta_tokens.shape == (N_META, d) and depth == 2
    tm = 512
    row = lambda a: a.reshape(1, -1)

    h = x.reshape(bsz * seq, d)
    hm = meta_tokens.astype(x.dtype)

    def ffn_pair(h, hm, i, s, need_meta=True):
        args = (row(ffn_norm_pre[i, s]), ffn_w1[i, s].astype(BF16), ffn_w3[i, s].astype(BF16),
                ffn_w2[i, s].astype(BF16), row(ffn_norm_post[i, s]))
        hm = _ffn(hm, *args, tm=N_META) if need_meta else None
        return _ffn(h, *args, tm=tm), hm

    h, hm = ffn_pair(h, hm, 0, 0)
    pool_args = (row(mix_norm_pre[0]), pool_w[0].astype(BF16), row(pool_b[0]), row(pool_scale[0]),
                 row(mix_norm_post[0]))
    h_new = _pool(h, hm, *pool_args, seq=seq, tm=tm, is_meta=False)
    hm = _pool(hm, hm, *pool_args, seq=N_META, tm=N_META, is_meta=True)
    h, hm = ffn_pair(h_new, hm, 0, 1)

    h, hm = ffn_pair(h, hm, 1, 0)
    g_pre = row(mix_norm_pre[1])
    w_qkv = attn_w_qkv[0].astype(BF16)
    q, k, v = _qkv(h, g_pre, w_qkv, tm=tm)
    _, k_meta, v_meta = _qkv(hm, g_pre, w_qkv, tm=N_META)
    o = _attention(q, k, v, k_meta, v_meta, row(attn_lambda_q1[0]), row(attn_lambda_k1[0]),
                   row(attn_lambda_q2[0]), row(attn_lambda_k2[0]), row(attn_subln[0]),
                   seq=seq, tq=1024, tk=512, lambda_init=_lambda_init(1))
    h = _oproj(o, h, attn_w_o[0].astype(BF16), row(mix_norm_post[1]), tm=tm)
    h, _ = ffn_pair(h, None, 1, 1, need_meta=False)
    return h.reshape(bsz, seq, d)
```

```python
import functools
import math

import jax
import jax.numpy as jnp
from jax import lax
from jax.experimental import pallas as pl
from jax.experimental.pallas import tpu as pltpu

D_MODEL = 1024
N_META = 16
POOL_WINDOWS = (2, 4, 8, 16)
POOL_GROUP = D_MODEL // len(POOL_WINDOWS)
HEAD_DIM = 64
HEAD_WIDTH = 2 * HEAD_DIM
N_HEADS = D_MODEL // HEAD_WIDTH
EPS = 1e-6
HALO = 16
MASK_VALUE = float(jnp.finfo(jnp.float32).min)
LOG2_E = math.log2(math.e)

LANES = 128
MXU_TILE = 256
VMEM_LIMIT_BYTES = 56 * 1024 * 1024

F32 = jnp.float32
BF16 = jnp.bfloat16


def _lambda_init(layer_idx):
    return 0.8 - 0.6 * math.exp(-0.3 * layer_idx)


def _rmsnorm(x, g):
    return x * lax.rsqrt(jnp.mean(x * x, axis=-1, keepdims=True) + EPS) * g


def _resident(shape):
    return pl.BlockSpec(shape, lambda *_: (0,) * len(shape), pipeline_mode=pl.Buffered(1))


def _params(semantics):
    return pltpu.CompilerParams(dimension_semantics=semantics, vmem_limit_bytes=VMEM_LIMIT_BYTES)


def _ffn_kernel(h_ref, gpre_ref, w1_ref, w3_ref, w2_ref, gpost_ref, o_ref, acc_ref, *, f_chunks):
    h = h_ref[...]
    hn = _rmsnorm(h, gpre_ref[...]).astype(BF16)
    for idx, (lo, size) in enumerate(f_chunks):
        a = jnp.dot(hn, w1_ref[:, lo:lo + size], preferred_element_type=F32)
        b = jnp.dot(hn, w3_ref[:, lo:lo + size], preferred_element_type=F32)
        g = (jax.nn.silu(a) * b).astype(BF16)
        part = jnp.dot(g, w2_ref[lo:lo + size, :], preferred_element_type=F32)
        if idx == 0:
            acc_ref[...] = part
        else:
            acc_ref[...] += part
    o_ref[...] = h + 0.5 * _rmsnorm(acc_ref[...], gpost_ref[...])


def _ffn(h, gpre, w1, w3, w2, gpost, *, tm):
    n, d = h.shape
    f = w1.shape[1]
    f_chunks = tuple((lo, min(1024, f - lo)) for lo in range(0, f, 1024))
    row = pl.BlockSpec((tm, d), lambda i: (i, 0))
    return pl.pallas_call(
        functools.partial(_ffn_kernel, f_chunks=f_chunks),
        out_shape=jax.ShapeDtypeStruct((n, d), F32),
        grid=(n // tm,),
        in_specs=[row, _resident((1, d)), _resident((d, f)), _resident((d, f)), _resident((f, d)),
                  _resident((1, d))],
        out_specs=row,
        scratch_shapes=[pltpu.VMEM((tm, d), F32)],
        compiler_params=_params(("parallel",)),
        name="ffn",
    )(h, gpre, w1, w3, w2, gpost)


def _pool_kernel(h_ref, prev_ref, meta_ref, gpre_ref, w_ref, b_ref, scale_ref, gpost_ref, o_ref, *,
                 is_meta):
    rows = h_ref.shape[0]
    h = h_ref[...]
    gpre = gpre_ref[...]
    hn = _rmsnorm(h, gpre)
    if is_meta:
        halo_n = jnp.zeros((HALO, D_MODEL), F32)
    else:
        halo = jnp.where(pl.program_id(1) == 0, meta_ref[...], prev_ref[...])
        halo_n = _rmsnorm(halo, gpre)
    ext = jnp.concatenate([halo_n, hn], axis=0)
    ys = []
    for gi, win in enumerate(POOL_WINDOWS):
        cols = slice(gi * POOL_GROUP, (gi + 1) * POOL_GROUP)
        acc = ext[:, cols]
        step = 1
        while step < win:
            acc = acc + pltpu.roll(acc, step, axis=0)
            step *= 2
        wsum = acc[HALO:]
        if is_meta:
            pos = lax.broadcasted_iota(jnp.int32, (rows, POOL_GROUP), 0)
            pooled = wsum / jnp.minimum(pos + 1, win).astype(F32)
        else:
            pooled = wsum * (1.0 / win)
        diff = (pooled - hn[:, cols]).astype(BF16)
        y = jnp.dot(diff, w_ref[gi], preferred_element_type=F32) + b_ref[:, cols]
        ys.append(y * scale_ref[:, cols])
    m = jnp.concatenate(ys, axis=1)
    o_ref[...] = h + _rmsnorm(m, gpost_ref[...])


def _pool(h, h_meta, gpre, w, b, scale, gpost, *, seq, tm, is_meta):
    n, d = h.shape
    if is_meta:
        grid = (1, 1)
        row = pl.BlockSpec((n, d), lambda bi, i: (0, 0))
        prev = pl.BlockSpec((HALO, d), lambda bi, i: (0, 0))
    else:
        grid = (n // seq, seq // tm)
        row = pl.BlockSpec((tm, d), lambda bi, i: (bi * (seq // tm) + i, 0))
        prev = pl.BlockSpec(
            (HALO, d), lambda bi, i: (jnp.maximum(bi * (seq // HALO) + i * (tm // HALO) - 1, 0), 0))
    g = len(POOL_WINDOWS)
    return pl.pallas_call(
        functools.partial(_pool_kernel, is_meta=is_meta),
        out_shape=jax.ShapeDtypeStruct((n, d), F32),
        grid=grid,
        in_specs=[row, prev, _resident((N_META, d)), _resident((1, d)),
                  _resident((g, POOL_GROUP, POOL_GROUP)), _resident((1, d)), _resident((1, d)),
                  _resident((1, d))],
        out_specs=row,
        compiler_params=_params(("parallel", "parallel")),
        name="pool_meta" if is_meta else "pool",
    )(h, h, h_meta, gpre, w, b, scale, gpost)


def _qkv_kernel(h_ref, g_ref, w_ref, q_ref, k_ref, v_ref):
    hn = _rmsnorm(h_ref[...], g_ref[...]).astype(BF16)
    d = D_MODEL
    q = jnp.dot(hn, w_ref[:, 0:d], preferred_element_type=F32)
    q_ref[...] = (q * (HEAD_DIM ** -0.5 * LOG2_E)).astype(BF16)
    k_ref[...] = jnp.dot(hn, w_ref[:, d:2 * d], preferred_element_type=F32).astype(BF16)
    v_ref[...] = jnp.dot(hn, w_ref[:, 2 * d:3 * d], preferred_element_type=F32).astype(BF16)


def _qkv(h, g, w, *, tm):
    n, d = h.shape
    row = pl.BlockSpec((tm, d), lambda i: (i, 0))
    out = jax.ShapeDtypeStruct((n, d), BF16)
    return pl.pallas_call(
        _qkv_kernel,
        out_shape=(out, out, out),
        grid=(n // tm,),
        in_specs=[row, _resident((1, d)), _resident((d, 3 * d))],
        out_specs=(row, row, row),
        compiler_params=_params(("parallel",)),
        name="qkv",
    )(h, g, w)


def _attn_kernel(q_ref, k_ref, v_ref, km_ref, vm_ref, lq1_ref, lk1_ref, lq2_ref, lk2_ref, subln_ref,
                 o_ref, qq_ref, vx_ref, kmx_ref, vmx_ref, m_ref, acc_ref, *, tq, tk, lambda_init):
    qi = pl.program_id(2)
    nt = (((1,), (1,)), ((), ()))
    hw = HEAD_WIDTH

    @pl.when(qi == 0)
    def _stage_keys_values():
        vx_ref[:, 0:hw] = v_ref[...]
        vx_ref[:, hw:2 * hw] = jnp.ones((v_ref.shape[0], hw), BF16)
        kmx_ref[...] = jnp.zeros(kmx_ref.shape, BF16)
        kmx_ref[0:N_META, :] = km_ref[...]
        vmx_ref[...] = jnp.zeros(vmx_ref.shape, BF16)
        vmx_ref[0:N_META, 0:hw] = vm_ref[...]
        vmx_ref[0:N_META, hw:2 * hw] = jnp.ones((N_META, hw), BF16)

    q = q_ref[...]
    lane = lax.broadcasted_iota(jnp.int32, q.shape, 1)
    zero = jnp.zeros_like(q)
    qq_ref[0:tq, :] = jnp.where(lane < HEAD_DIM, q, zero)
    qq_ref[tq:2 * tq, :] = jnp.where(lane >= HEAD_DIM, q, zero)

    def update(r0, nrows, k, vx, masks, first=False):
        rows = slice(r0, r0 + nrows)
        s = lax.dot_general(qq_ref[rows, :], k, nt, preferred_element_type=F32)
        blocks = []
        for j, keep in enumerate(masks):
            blk = s[:, j * LANES:(j + 1) * LANES]
            blocks.append(blk if keep is None else jnp.where(keep, blk, MASK_VALUE))
        m_tile = jnp.max(functools.reduce(jnp.maximum, blocks), axis=1, keepdims=True)
        if first:
            m_new = jnp.broadcast_to(m_tile, (nrows, LANES))
        else:
            m_prev = m_ref[rows, :]
            m_new = jnp.maximum(m_prev, m_tile)
            alpha = jnp.exp2(m_prev - m_new)
        p = jnp.concatenate([jnp.exp2(blk - m_new) for blk in blocks], axis=1).astype(BF16)
        pv = jnp.dot(p, vx, preferred_element_type=F32)
        if first:
            acc_ref[rows, :] = pv
        else:
            acc_ref[rows, :] = jnp.concatenate([alpha, alpha], axis=1) * acc_ref[rows, :] + pv
        m_ref[rows, :] = m_new

    def query_tile(i):
        start = i * tq
        meta_cols = lax.broadcasted_iota(jnp.int32, (2 * tq, LANES), 1)
        update(0, 2 * tq, kmx_ref[...], vmx_ref[...], [meta_cols < N_META], first=True)
        for lo in range(0, start, 2 * tk):
            for r0 in range(0, 2 * tq, tk):
                update(r0, tk, k_ref[lo:lo + 2 * tk, :], vx_ref[lo:lo + 2 * tk, :],
                       [None] * (2 * tk // LANES))
        for lo in range(0, tq, tk):
            nrows = tq - lo
            row_id = lax.broadcasted_iota(jnp.int32, (nrows, LANES), 0)
            col_id = lax.broadcasted_iota(jnp.int32, (nrows, LANES), 1)
            tri = [col_id + j * LANES <= row_id for j in range(tk // LANES)]
            for half in range(2):
                update(half * tq + lo, nrows, k_ref[start + lo:start + lo + tk, :],
                       vx_ref[start + lo:start + lo + tk, :], tri)

    for i in range(k_ref.shape[0] // tq):
        pl.when(qi == i)(functools.partial(query_tile, i))

    lam = (jnp.exp(jnp.sum(lq1_ref[...] * lk1_ref[...], axis=1, keepdims=True))
           - jnp.exp(jnp.sum(lq2_ref[...] * lk2_ref[...], axis=1, keepdims=True)) + lambda_init)
    acc = acc_ref[...]
    o = acc[:, 0:hw] / acc[:, hw:2 * hw]
    o = o[0:tq] - lam * o[tq:2 * tq]
    o_ref[...] = (_rmsnorm(o, subln_ref[...]) * (1.0 - lambda_init)).astype(o_ref.dtype)


def _attention(q, k, v, k_meta, v_meta, lq1, lk1, lq2, lk2, subln, *, seq, tq, tk, lambda_init):
    n, d = q.shape
    nq = seq // tq
    hw = HEAD_WIDTH
    lam_spec = _resident((1, HEAD_DIM))
    return pl.pallas_call(
        functools.partial(_attn_kernel, tq=tq, tk=tk, lambda_init=lambda_init),
        out_shape=jax.ShapeDtypeStruct((n, d), BF16),
        grid=(n // seq, N_HEADS, nq),
        in_specs=[
            pl.BlockSpec((tq, hw), lambda b, h, i: (b * nq + i, h)),
            pl.BlockSpec((seq, hw), lambda b, h, i: (b, h)),
            pl.BlockSpec((seq, hw), lambda b, h, i: (b, h)),
            pl.BlockSpec((N_META, hw), lambda b, h, i: (0, h)),
            pl.BlockSpec((N_META, hw), lambda b, h, i: (0, h)),
            lam_spec, lam_spec, lam_spec, lam_spec, _resident((1, hw)),
        ],
        out_specs=pl.BlockSpec((tq, hw), lambda b, h, i: (b * nq + i, h)),
        scratch_shapes=[
            pltpu.VMEM((2 * tq, hw), BF16),
            pltpu.VMEM((seq, 2 * hw), BF16),
            pltpu.VMEM((LANES, hw), BF16),
            pltpu.VMEM((LANES, 2 * hw), BF16),
            pltpu.VMEM((2 * tq, LANES), F32),
            pltpu.VMEM((2 * tq, 2 * hw), F32),
        ],
        compiler_params=_params(("parallel", "parallel", "arbitrary")),
        name="diff_attn",
    )(q, k, v, k_meta, v_meta, lq1, lk1, lq2, lk2, subln)


def _oproj_kernel(o_ref, h_ref, w_ref, g_ref, out_ref):
    m = jnp.dot(o_ref[...], w_ref[...], preferred_element_type=F32)
    out_ref[...] = h_ref[...] + _rmsnorm(m, g_ref[...])


def _oproj(o, h, w, g, *, tm):
    n, d = h.shape
    row = pl.BlockSpec((tm, d), lambda i: (i, 0))
    return pl.pallas_call(
        _oproj_kernel,
        out_shape=jax.ShapeDtypeStruct((n, d), F32),
        grid=(n // tm,),
        in_specs=[row, row, _resident((d, d)), _resident((1, d))],
        out_specs=row,
        compiler_params=_params(("parallel",)),
        name="oproj",
    )(o, h, w, g)


def kernel(x, meta_tokens, ffn_norm_pre, ffn_norm_post, ffn_w1, ffn_w3, ffn_w2, mix_norm_pre,
           mix_norm_post, pool_w, pool_b, pool_scale, attn_w_qkv, attn_w_o, attn_lambda_q1,
           attn_lambda_k1, attn_lambda_q2, attn_lambda_k2, attn_subln):
    bsz, seq, d = x.shape
    depth = ffn_w1.shape[0]
    assert d == D_MODEL and meta_tokens.shape == (N_META, d) and depth == 2
    tm = 512
    row = lambda a: a.reshape(1, -1)

    h = x.reshape(bsz * seq, d)
    hm = meta_tokens.astype(x.dtype)

    def ffn_pair(h, hm, i, s, need_meta=True):
        args = (row(ffn_norm_pre[i, s]), ffn_w1[i, s].astype(BF16), ffn_w3[i, s].astype(BF16),
                ffn_w2[i, s].astype(BF16), row(ffn_norm_post[i, s]))
        hm = _ffn(hm, *args, tm=N_META) if need_meta else None
        return _ffn(h, *args, tm=tm), hm

    h, hm = ffn_pair(h, hm, 0, 0)
    pool_args = (row(mix_norm_pre[0]), pool_w[0].astype(BF16), row(pool_b[0]), row(pool_scale[0]),
                 row(mix_norm_post[0]))
    h_new = _pool(h, hm, *pool_args, seq=seq, tm=tm, is_meta=False)
    hm = _pool(hm, hm, *pool_args, seq=N_META, tm=N_META, is_meta=True)
    h, hm = ffn_pair(h_new, hm, 0, 1)

    h, hm = ffn_pair(h, hm, 1, 0)
    g_pre = row(mix_norm_pre[1])
    w_qkv = attn_w_qkv[0].astype(BF16)
    q, k, v = _qkv(h, g_pre, w_qkv, tm=tm)
    _, k_meta, v_meta = _qkv(hm, g_pre, w_qkv, tm=N_META)
    o = _attention(q, k, v, k_meta, v_meta, row(attn_lambda_q1[0]), row(attn_lambda_k1[0]),
                   row(attn_lambda_q2[0]), row(attn_lambda_k2[0]), row(attn_subln[0]),
                   seq=seq, tq=1024, tk=MXU_TILE, lambda_init=_lambda_init(1))
    h = _oproj(o, h, attn_w_o[0].astype(BF16), row(mix_norm_post[1]), tm=tm)
    h, _ = ffn_pair(h, None, 1, 1, need_meta=False)
    return h.reshape(bsz, seq, d)
```

```python
import functools
import math

import jax
import jax.numpy as jnp
from jax import lax
from jax.experimental import pallas as pl
from jax.experimental.pallas import tpu as pltpu

D_MODEL = 1024
N_META = 16
POOL_WINDOWS = (2, 4, 8, 16)
POOL_GROUP = D_MODEL // len(POOL_WINDOWS)
HEAD_DIM = 64
HEAD_WIDTH = 2 * HEAD_DIM
N_HEADS = D_MODEL // HEAD_WIDTH
EPS = 1e-6
HALO = 16
MASK_VALUE = float(jnp.finfo(jnp.float32).min)
LOG2_E = math.log2(math.e)

LANES = 128
MXU_TILE = 256
VMEM_LIMIT_BYTES = 56 * 1024 * 1024

F32 = jnp.float32
BF16 = jnp.bfloat16


def _lambda_init(layer_idx):
    return 0.8 - 0.6 * math.exp(-0.3 * layer_idx)


def _rmsnorm(x, g):
    return x * lax.rsqrt(jnp.mean(x * x, axis=-1, keepdims=True) + EPS) * g


def _resident(shape):
    return pl.BlockSpec(shape, lambda *_: (0,) * len(shape), pipeline_mode=pl.Buffered(1))


def _params(semantics):
    return pltpu.CompilerParams(dimension_semantics=semantics, vmem_limit_bytes=VMEM_LIMIT_BYTES)


def _ffn_kernel(h_ref, gpre_ref, w1_ref, w3_ref, w2_ref, gpost_ref, o_ref, acc_ref, *, f_chunks, sub):
    for r0 in range(0, h_ref.shape[0], sub):
        rows = slice(r0, r0 + sub)
        h = h_ref[rows, :]
        r = lax.rsqrt(jnp.mean(h * h, axis=-1, keepdims=True) + EPS)
        hg = (h * gpre_ref[...]).astype(BF16)
        for idx, (lo, size) in enumerate(f_chunks):
            a = jnp.dot(hg, w1_ref[:, lo:lo + size], preferred_element_type=F32) * r
            b = jnp.dot(hg, w3_ref[:, lo:lo + size], preferred_element_type=F32) * r
            g = (jax.nn.silu(a) * b).astype(BF16)
            part = jnp.dot(g, w2_ref[lo:lo + size, :], preferred_element_type=F32)
            if idx == 0:
                acc_ref[rows, :] = part
            else:
                acc_ref[rows, :] += part
        o_ref[rows, :] = h + 0.5 * _rmsnorm(acc_ref[rows, :], gpost_ref[...])


def _ffn(h, gpre, w1, w3, w2, gpost, *, layer, slot, tm, sub):
    n, d = h.shape
    f = w1.shape[-1]
    f_chunks = tuple((lo, min(1024, f - lo)) for lo in range(0, f, 1024))
    row = pl.BlockSpec((tm, d), lambda i: (i, 0))

    def weight(rows, cols):
        return pl.BlockSpec((None, None, rows, cols), lambda i: (layer, slot, 0, 0),
                            pipeline_mode=pl.Buffered(1))

    return pl.pallas_call(
        functools.partial(_ffn_kernel, f_chunks=f_chunks, sub=sub),
        out_shape=jax.ShapeDtypeStruct((n, d), F32),
        grid=(n // tm,),
        in_specs=[row, _resident((1, d)), weight(d, f), weight(d, f), weight(f, d), _resident((1, d))],
        out_specs=row,
        scratch_shapes=[pltpu.VMEM((tm, d), F32)],
        compiler_params=_params(("parallel",)),
        name="ffn",
    )(h, gpre, w1, w3, w2, gpost)


def _pool_kernel(h_ref, prev_ref, meta_ref, gpre_ref, w_ref, b_ref, scale_ref, gpost_ref, o_ref, *,
                 is_meta):
    rows = h_ref.shape[0]
    h = h_ref[...]
    gpre = gpre_ref[...]
    hn = _rmsnorm(h, gpre)
    if is_meta:
        halo_n = jnp.zeros((HALO, D_MODEL), F32)
    else:
        halo = jnp.where(pl.program_id(1) == 0, meta_ref[...], prev_ref[...])
        halo_n = _rmsnorm(halo, gpre)
    ext = jnp.concatenate([halo_n, hn], axis=0)
    ys = []
    for gi, win in enumerate(POOL_WINDOWS):
        cols = slice(gi * POOL_GROUP, (gi + 1) * POOL_GROUP)
        acc = ext[:, cols]
        step = 1
        while step < win:
            acc = acc + pltpu.roll(acc, step, axis=0)
            step *= 2
        wsum = acc[HALO:]
        if is_meta:
            pos = lax.broadcasted_iota(jnp.int32, (rows, POOL_GROUP), 0)
            pooled = wsum / jnp.minimum(pos + 1, win).astype(F32)
        else:
            pooled = wsum * (1.0 / win)
        diff = (pooled - hn[:, cols]).astype(BF16)
        y = jnp.dot(diff, w_ref[gi], preferred_element_type=F32) + b_ref[:, cols]
        ys.append(y * scale_ref[:, cols])
    m = jnp.concatenate(ys, axis=1)
    o_ref[...] = h + _rmsnorm(m, gpost_ref[...])


def _pool(h, h_meta, gpre, w, b, scale, gpost, *, seq, tm, is_meta):
    n, d = h.shape
    if is_meta:
        grid = (1, 1)
        row = pl.BlockSpec((n, d), lambda bi, i: (0, 0))
        prev = pl.BlockSpec((HALO, d), lambda bi, i: (0, 0))
    else:
        grid = (n // seq, seq // tm)
        row = pl.BlockSpec((tm, d), lambda bi, i: (bi * (seq // tm) + i, 0))
        prev = pl.BlockSpec(
            (HALO, d), lambda bi, i: (jnp.maximum(bi * (seq // HALO) + i * (tm // HALO) - 1, 0), 0))
    g = len(POOL_WINDOWS)
    return pl.pallas_call(
        functools.partial(_pool_kernel, is_meta=is_meta),
        out_shape=jax.ShapeDtypeStruct((n, d), F32),
        grid=grid,
        in_specs=[row, prev, _resident((N_META, d)), _resident((1, d)),
                  _resident((g, POOL_GROUP, POOL_GROUP)), _resident((1, d)), _resident((1, d)),
                  _resident((1, d))],
        out_specs=row,
        compiler_params=_params(("parallel", "parallel")),
        name="pool_meta" if is_meta else "pool",
    )(h, h, h_meta, gpre, w, b, scale, gpost)


def _qkv_kernel(h_ref, g_ref, w_ref, q_ref, k_ref, v_ref):
    hn = _rmsnorm(h_ref[...], g_ref[...]).astype(BF16)
    d = D_MODEL
    q = jnp.dot(hn, w_ref[:, 0:d], preferred_element_type=F32)
    q_ref[...] = (q * (HEAD_DIM ** -0.5 * LOG2_E)).astype(BF16)
    k_ref[...] = jnp.dot(hn, w_ref[:, d:2 * d], preferred_element_type=F32).astype(BF16)
    v_ref[...] = jnp.dot(hn, w_ref[:, 2 * d:3 * d], preferred_element_type=F32).astype(BF16)


def _qkv(h, g, w, *, tm):
    n, d = h.shape
    row = pl.BlockSpec((tm, d), lambda i: (i, 0))
    out = jax.ShapeDtypeStruct((n, d), BF16)
    return pl.pallas_call(
        _qkv_kernel,
        out_shape=(out, out, out),
        grid=(n // tm,),
        in_specs=[row, _resident((1, d)), _resident((d, 3 * d))],
        out_specs=(row, row, row),
        compiler_params=_params(("parallel",)),
        name="qkv",
    )(h, g, w)


def _attn_kernel(q_ref, k_ref, v_ref, km_ref, vm_ref, lq1_ref, lk1_ref, lq2_ref, lk2_ref, subln_ref,
                 o_ref, qq_ref, vx_ref, kmx_ref, vmx_ref, m_ref, acc_ref, *, tq, tk, lambda_init):
    qi = pl.program_id(2)
    nt = (((1,), (1,)), ((), ()))
    hw = HEAD_WIDTH

    @pl.when(qi == 0)
    def _stage_keys_values():
        vx_ref[:, 0:hw] = v_ref[...]
        vx_ref[:, hw:2 * hw] = jnp.ones((v_ref.shape[0], hw), BF16)
        kmx_ref[...] = jnp.zeros(kmx_ref.shape, BF16)
        kmx_ref[0:N_META, :] = km_ref[...]
        vmx_ref[...] = jnp.zeros(vmx_ref.shape, BF16)
        vmx_ref[0:N_META, 0:hw] = vm_ref[...]
        vmx_ref[0:N_META, hw:2 * hw] = jnp.ones((N_META, hw), BF16)

    q = q_ref[...]
    lane = lax.broadcasted_iota(jnp.int32, q.shape, 1)
    zero = jnp.zeros_like(q)
    qq_ref[0:tq, :] = jnp.where(lane < HEAD_DIM, q, zero)
    qq_ref[tq:2 * tq, :] = jnp.where(lane >= HEAD_DIM, q, zero)

    def update(r0, nrows, k, vx, masks, first=False):
        rows = slice(r0, r0 + nrows)
        s = lax.dot_general(qq_ref[rows, :], k, nt, preferred_element_type=F32)
        blocks = []
        for j, keep in enumerate(masks):
            blk = s[:, j * LANES:(j + 1) * LANES]
            blocks.append(blk if keep is None else jnp.where(keep, blk, MASK_VALUE))
        m_tile = jnp.max(functools.reduce(jnp.maximum, blocks), axis=1, keepdims=True)
        if first:
            m_new = jnp.broadcast_to(m_tile, (nrows, LANES))
        else:
            m_prev = m_ref[rows, :]
            m_new = jnp.maximum(m_prev, m_tile)
            alpha = jnp.exp2(m_prev - m_new)
        p = jnp.concatenate([jnp.exp2(blk - m_new) for blk in blocks], axis=1).astype(BF16)
        pv = jnp.dot(p, vx, preferred_element_type=F32)
        if first:
            acc_ref[rows, :] = pv
        else:
            acc_ref[rows, :] = jnp.concatenate([alpha, alpha], axis=1) * acc_ref[rows, :] + pv
        m_ref[rows, :] = m_new

    def query_tile(i):
        start = i * tq
        meta_cols = lax.broadcasted_iota(jnp.int32, (2 * tq, LANES), 1)
        update(0, 2 * tq, kmx_ref[...], vmx_ref[...], [meta_cols < N_META], first=True)
        for lo in range(0, start, 2 * tk):
            for r0 in range(0, 2 * tq, tk):
                update(r0, tk, k_ref[lo:lo + 2 * tk, :], vx_ref[lo:lo + 2 * tk, :],
                       [None] * (2 * tk // LANES))
        for lo in range(0, tq, tk):
            nrows = tq - lo
            row_id = lax.broadcasted_iota(jnp.int32, (nrows, LANES), 0)
            col_id = lax.broadcasted_iota(jnp.int32, (nrows, LANES), 1)
            tri = [col_id + j * LANES <= row_id for j in range(tk // LANES)]
            for half in range(2):
                update(half * tq + lo, nrows, k_ref[start + lo:start + lo + tk, :],
                       vx_ref[start + lo:start + lo + tk, :], tri)

    for i in range(k_ref.shape[0] // tq):
        pl.when(qi == i)(functools.partial(query_tile, i))

    lam = (jnp.exp(jnp.sum(lq1_ref[...] * lk1_ref[...], axis=1, keepdims=True))
           - jnp.exp(jnp.sum(lq2_ref[...] * lk2_ref[...], axis=1, keepdims=True)) + lambda_init)
    acc = acc_ref[...]
    o = acc[:, 0:hw] / acc[:, hw:2 * hw]
    o = o[0:tq] - lam * o[tq:2 * tq]
    o_ref[...] = (_rmsnorm(o, subln_ref[...]) * (1.0 - lambda_init)).astype(o_ref.dtype)


def _attention(q, k, v, k_meta, v_meta, lq1, lk1, lq2, lk2, subln, *, seq, tq, tk, lambda_init):
    n, d = q.shape
    nq = seq // tq
    hw = HEAD_WIDTH
    lam_spec = _resident((1, HEAD_DIM))
    return pl.pallas_call(
        functools.partial(_attn_kernel, tq=tq, tk=tk, lambda_init=lambda_init),
        out_shape=jax.ShapeDtypeStruct((n, d), BF16),
        grid=(n // seq, N_HEADS, nq),
        in_specs=[
            pl.BlockSpec((tq, hw), lambda b, h, i: (b * nq + i, h)),
            pl.BlockSpec((seq, hw), lambda b, h, i: (b, h)),
            pl.BlockSpec((seq, hw), lambda b, h, i: (b, h)),
            pl.BlockSpec((N_META, hw), lambda b, h, i: (0, h)),
            pl.BlockSpec((N_META, hw), lambda b, h, i: (0, h)),
            lam_spec, lam_spec, lam_spec, lam_spec, _resident((1, hw)),
        ],
        out_specs=pl.BlockSpec((tq, hw), lambda b, h, i: (b * nq + i, h)),
        scratch_shapes=[
            pltpu.VMEM((2 * tq, hw), BF16),
            pltpu.VMEM((seq, 2 * hw), BF16),
            pltpu.VMEM((LANES, hw), BF16),
            pltpu.VMEM((LANES, 2 * hw), BF16),
            pltpu.VMEM((2 * tq, LANES), F32),
            pltpu.VMEM((2 * tq, 2 * hw), F32),
        ],
        compiler_params=_params(("parallel", "parallel", "arbitrary")),
        name="diff_attn",
    )(q, k, v, k_meta, v_meta, lq1, lk1, lq2, lk2, subln)


def _oproj_kernel(o_ref, h_ref, w_ref, g_ref, out_ref):
    m = jnp.dot(o_ref[...], w_ref[...], preferred_element_type=F32)
    out_ref[...] = h_ref[...] + _rmsnorm(m, g_ref[...])


def _oproj(o, h, w, g, *, tm):
    n, d = h.shape
    row = pl.BlockSpec((tm, d), lambda i: (i, 0))
    return pl.pallas_call(
        _oproj_kernel,
        out_shape=jax.ShapeDtypeStruct((n, d), F32),
        grid=(n // tm,),
        in_specs=[row, row, _resident((d, d)), _resident((1, d))],
        out_specs=row,
        compiler_params=_params(("parallel",)),
        name="oproj",
    )(o, h, w, g)


def kernel(x, meta_tokens, ffn_norm_pre, ffn_norm_post, ffn_w1, ffn_w3, ffn_w2, mix_norm_pre,
           mix_norm_post, pool_w, pool_b, pool_scale, attn_w_qkv, attn_w_o, attn_lambda_q1,
           attn_lambda_k1, attn_lambda_q2, attn_lambda_k2, attn_subln):
    bsz, seq, d = x.shape
    depth = ffn_w1.shape[0]
    assert d == D_MODEL and meta_tokens.shape == (N_META, d) and depth == 2
    tm = 512
    row = lambda a: a.reshape(1, -1)

    h = x.reshape(bsz * seq, d)
    hm = meta_tokens.astype(x.dtype)

    w1, w3, w2 = ffn_w1.astype(BF16), ffn_w3.astype(BF16), ffn_w2.astype(BF16)

    def ffn_pair(h, hm, i, s, need_meta=True):
        args = (row(ffn_norm_pre[i, s]), w1, w3, w2, row(ffn_norm_post[i, s]))
        hm = _ffn(hm, *args, layer=i, slot=s, tm=N_META, sub=N_META) if need_meta else None
        return _ffn(h, *args, layer=i, slot=s, tm=2 * tm, sub=tm), hm

    h, hm = ffn_pair(h, hm, 0, 0)
    pool_args = (row(mix_norm_pre[0]), pool_w[0].astype(BF16), row(pool_b[0]), row(pool_scale[0]),
                 row(mix_norm_post[0]))
    h_new = _pool(h, hm, *pool_args, seq=seq, tm=tm, is_meta=False)
    hm = _pool(hm, hm, *pool_args, seq=N_META, tm=N_META, is_meta=True)
    h, hm = ffn_pair(h_new, hm, 0, 1)

    h, hm = ffn_pair(h, hm, 1, 0)
    g_pre = row(mix_norm_pre[1])
    w_qkv = attn_w_qkv[0].astype(BF16)
    q, k, v = _qkv(h, g_pre, w_qkv, tm=tm)
    _, k_meta, v_meta = _qkv(hm, g_pre, w_qkv, tm=N_META)
    o = _attention(q, k, v, k_meta, v_meta, row(attn_lambda_q1[0]), row(attn_lambda_k1[0]),
                   row(attn_lambda_q2[0]), row(attn_lambda_k2[0]), row(attn_subln[0]),
                   seq=seq, tq=1024, tk=MXU_TILE, lambda_init=_lambda_init(1))
    h = _oproj(o, h, attn_w_o[0].astype(BF16), row(mix_norm_post[1]), tm=tm)
    h, _ = ffn_pair(h, None, 1, 1, need_meta=False)
    return h.reshape(bsz, seq, d)
```

```python
import functools
import math

import jax
import jax.numpy as jnp
from jax import lax
from jax.experimental import pallas as pl
from jax.experimental.pallas import tpu as pltpu

D_MODEL = 1024
N_META = 16
POOL_WINDOWS = (2, 4, 8, 16)
POOL_GROUP = D_MODEL // len(POOL_WINDOWS)
HEAD_DIM = 64
HEAD_WIDTH = 2 * HEAD_DIM
N_HEADS = D_MODEL // HEAD_WIDTH
EPS = 1e-6
HALO = 16
MASK_VALUE = float(jnp.finfo(jnp.float32).min)
LOG2_E = math.log2(math.e)

LANES = 128
MXU_TILE = 256
VMEM_LIMIT_BYTES = 56 * 1024 * 1024

F32 = jnp.float32
BF16 = jnp.bfloat16


def _lambda_init(layer_idx):
    return 0.8 - 0.6 * math.exp(-0.3 * layer_idx)


def _rmsnorm(x, g):
    return x * lax.rsqrt(jnp.mean(x * x, axis=-1, keepdims=True) + EPS) * g


def _resident(shape):
    return pl.BlockSpec(shape, lambda *_: (0,) * len(shape), pipeline_mode=pl.Buffered(1))


def _params(semantics):
    return pltpu.CompilerParams(dimension_semantics=semantics, vmem_limit_bytes=VMEM_LIMIT_BYTES)


N_MIXER_REFS = {None: 0, "pool": 7, "oproj": 3}


def _ffn_kernel(*refs, f_chunks, sub, mixer):
    mixer_refs = refs[:N_MIXER_REFS[mixer]]
    h_ref, gpre_ref, w1_ref, w3_ref, w2_ref, gpost_ref, o_ref, acc_ref = refs[N_MIXER_REFS[mixer]:]
    for r0 in range(0, h_ref.shape[0], sub):
        rows = slice(r0, r0 + sub)
        h = h_ref[rows, :]
        if mixer == "pool":
            prev_ref, meta_ref, mgpre_ref, pw_ref, pb_ref, pscale_ref, mgpost_ref = mixer_refs
            if r0 == 0:
                halo = jnp.where(pl.program_id(1) == 0, meta_ref[...], prev_ref[...])
            else:
                halo = h_ref[r0 - HALO:r0, :]
            h = _pool_mix(h, halo, mgpre_ref[...], pw_ref, pb_ref, pscale_ref, mgpost_ref[...])
        elif mixer == "oproj":
            attn_ref, wo_ref, mgpost_ref = mixer_refs
            m = jnp.dot(attn_ref[rows, :], wo_ref[...], preferred_element_type=F32)
            h = h + _rmsnorm(m, mgpost_ref[...])
        r = lax.rsqrt(jnp.mean(h * h, axis=-1, keepdims=True) + EPS)
        hg = (h * gpre_ref[...]).astype(BF16)
        for idx, (lo, size) in enumerate(f_chunks):
            a = jnp.dot(hg, w1_ref[:, lo:lo + size], preferred_element_type=F32) * r
            b = jnp.dot(hg, w3_ref[:, lo:lo + size], preferred_element_type=F32) * r
            g = (jax.nn.silu(a) * b).astype(BF16)
            part = jnp.dot(g, w2_ref[lo:lo + size, :], preferred_element_type=F32)
            if idx == 0:
                acc_ref[rows, :] = part
            else:
                acc_ref[rows, :] += part
        o_ref[rows, :] = h + 0.5 * _rmsnorm(acc_ref[rows, :], gpost_ref[...])


def _ffn(h, gpre, w1, w3, w2, gpost, *, layer, slot, tm, sub, seq=None, mixer=None, mixer_args=()):
    n, d = h.shape
    f = w1.shape[-1]
    seq = n if seq is None else seq
    tiles = seq // tm
    f_chunks = tuple((lo, min(1024, f - lo)) for lo in range(0, f, 1024))
    row = pl.BlockSpec((tm, d), lambda b, i: (b * tiles + i, 0))
    vec = _resident((1, d))

    def weight(rows, cols):
        return pl.BlockSpec((None, None, rows, cols), lambda b, i: (layer, slot, 0, 0),
                            pipeline_mode=pl.Buffered(1))

    if mixer == "pool":
        prev = pl.BlockSpec(
            (HALO, d), lambda b, i: (jnp.maximum(b * (seq // HALO) + i * (tm // HALO) - 1, 0), 0))
        mixer_ops = (h,) + tuple(mixer_args)
        mixer_specs = [prev, _resident((N_META, d)), vec, _resident(mixer_args[2].shape), vec, vec, vec]
    elif mixer == "oproj":
        mixer_ops = tuple(mixer_args)
        mixer_specs = [row, _resident((d, d)), vec]
    else:
        mixer_ops, mixer_specs = (), []
    assert len(mixer_specs) == N_MIXER_REFS[mixer]

    return pl.pallas_call(
        functools.partial(_ffn_kernel, f_chunks=f_chunks, sub=sub, mixer=mixer),
        out_shape=jax.ShapeDtypeStruct((n, d), F32),
        grid=(n // seq, tiles),
        in_specs=mixer_specs + [row, vec, weight(d, f), weight(d, f), weight(f, d), vec],
        out_specs=row,
        scratch_shapes=[pltpu.VMEM((tm, d), F32)],
        compiler_params=_params(("parallel", "parallel")),
        name="ffn" if mixer is None else "ffn_" + mixer,
    )(*mixer_ops, h, gpre, w1, w3, w2, gpost)


def _pool_mix(h, halo, gpre, w_ref, b_ref, scale_ref, gpost):
    rows = h.shape[0]
    is_meta = halo is None
    hn = _rmsnorm(h, gpre)
    halo_n = jnp.zeros((HALO, D_MODEL), F32) if is_meta else _rmsnorm(halo, gpre)
    ext = jnp.concatenate([halo_n, hn], axis=0)
    ys = []
    for gi, win in enumerate(POOL_WINDOWS):
        cols = slice(gi * POOL_GROUP, (gi + 1) * POOL_GROUP)
        acc = ext[:, cols]
        step = 1
        while step < win:
            acc = acc + pltpu.roll(acc, step, axis=0)
            step *= 2
        wsum = acc[HALO:]
        if is_meta:
            pos = lax.broadcasted_iota(jnp.int32, (rows, POOL_GROUP), 0)
            pooled = wsum / jnp.minimum(pos + 1, win).astype(F32)
        else:
            pooled = wsum * (1.0 / win)
        diff = (pooled - hn[:, cols]).astype(BF16)
        y = jnp.dot(diff, w_ref[gi], preferred_element_type=F32) + b_ref[:, cols]
        ys.append(y * scale_ref[:, cols])
    m = jnp.concatenate(ys, axis=1)
    return h + _rmsnorm(m, gpost)


def _pool_meta_kernel(h_ref, gpre_ref, w_ref, b_ref, scale_ref, gpost_ref, o_ref):
    o_ref[...] = _pool_mix(h_ref[...], None, gpre_ref[...], w_ref, b_ref, scale_ref, gpost_ref[...])


def _pool_meta(h, gpre, w, b, scale, gpost):
    n, d = h.shape
    vec = _resident((1, d))
    return pl.pallas_call(
        _pool_meta_kernel,
        out_shape=jax.ShapeDtypeStruct((n, d), F32),
        grid=(1,),
        in_specs=[_resident((n, d)), vec, _resident(w.shape), vec, vec, vec],
        out_specs=pl.BlockSpec((n, d), lambda i: (0, 0)),
        compiler_params=_params(("arbitrary",)),
        name="pool_meta",
    )(h, gpre, w, b, scale, gpost)


def _qkv_kernel(h_ref, g_ref, w_ref, q_ref, k_ref, v_ref):
    hn = _rmsnorm(h_ref[...], g_ref[...]).astype(BF16)
    d = D_MODEL
    q = jnp.dot(hn, w_ref[:, 0:d], preferred_element_type=F32)
    q_ref[...] = (q * (HEAD_DIM ** -0.5 * LOG2_E)).astype(BF16)
    k_ref[...] = jnp.dot(hn, w_ref[:, d:2 * d], preferred_element_type=F32).astype(BF16)
    v_ref[...] = jnp.dot(hn, w_ref[:, 2 * d:3 * d], preferred_element_type=F32).astype(BF16)


def _qkv(h, g, w, *, tm):
    n, d = h.shape
    row = pl.BlockSpec((tm, d), lambda i: (i, 0))
    out = jax.ShapeDtypeStruct((n, d), BF16)
    return pl.pallas_call(
        _qkv_kernel,
        out_shape=(out, out, out),
        grid=(n // tm,),
        in_specs=[row, _resident((1, d)), _resident((d, 3 * d))],
        out_specs=(row, row, row),
        compiler_params=_params(("parallel",)),
        name="qkv",
    )(h, g, w)


def _attn_kernel(q_ref, k_ref, v_ref, km_ref, vm_ref, lq1_ref, lk1_ref, lq2_ref, lk2_ref, subln_ref,
                 o_ref, qq_ref, vx_ref, kmx_ref, vmx_ref, m_ref, acc_ref, *, tq, tk, lambda_init):
    qi = pl.program_id(2)
    nt = (((1,), (1,)), ((), ()))
    hw = HEAD_WIDTH

    @pl.when(qi == 0)
    def _stage_keys_values():
        vx_ref[:, 0:hw] = v_ref[...]
        vx_ref[:, hw:2 * hw] = jnp.ones((v_ref.shape[0], hw), BF16)
        kmx_ref[...] = jnp.zeros(kmx_ref.shape, BF16)
        kmx_ref[0:N_META, :] = km_ref[...]
        vmx_ref[...] = jnp.zeros(vmx_ref.shape, BF16)
        vmx_ref[0:N_META, 0:hw] = vm_ref[...]
        vmx_ref[0:N_META, hw:2 * hw] = jnp.ones((N_META, hw), BF16)

    q = q_ref[...]
    lane = lax.broadcasted_iota(jnp.int32, q.shape, 1)
    zero = jnp.zeros_like(q)
    qq_ref[0:tq, :] = jnp.where(lane < HEAD_DIM, q, zero)
    qq_ref[tq:2 * tq, :] = jnp.where(lane >= HEAD_DIM, q, zero)

    def update(r0, nrows, k, vx, masks, first=False):
        rows = slice(r0, r0 + nrows)
        s = lax.dot_general(qq_ref[rows, :], k, nt, preferred_element_type=F32)
        blocks = []
        for j, keep in enumerate(masks):
            blk = s[:, j * LANES:(j + 1) * LANES]
            blocks.append(blk if keep is None else jnp.where(keep, blk, MASK_VALUE))
        m_tile = jnp.max(functools.reduce(jnp.maximum, blocks), axis=1, keepdims=True)
        if first:
            m_new = jnp.broadcast_to(m_tile, (nrows, LANES))
        else:
            m_prev = m_ref[rows, :]
            m_new = jnp.maximum(m_prev, m_tile)
            alpha = jnp.exp2(m_prev - m_new)
        p = jnp.concatenate([jnp.exp2(blk - m_new) for blk in blocks], axis=1).astype(BF16)
        pv = jnp.dot(p, vx, preferred_element_type=F32)
        if first:
            acc_ref[rows, :] = pv
        else:
            acc_ref[rows, :] = jnp.concatenate([alpha, alpha], axis=1) * acc_ref[rows, :] + pv
        m_ref[rows, :] = m_new

    def query_tile(i):
        start = i * tq
        meta_cols = lax.broadcasted_iota(jnp.int32, (2 * tq, LANES), 1)
        update(0, 2 * tq, kmx_ref[...], vmx_ref[...], [meta_cols < N_META], first=True)
        for lo in range(0, start, 2 * tk):
            for r0 in range(0, 2 * tq, tk):
                update(r0, tk, k_ref[lo:lo + 2 * tk, :], vx_ref[lo:lo + 2 * tk, :],
                       [None] * (2 * tk // LANES))
        for lo in range(0, tq, tk):
            nrows = tq - lo
            row_id = lax.broadcasted_iota(jnp.int32, (nrows, LANES), 0)
            col_id = lax.broadcasted_iota(jnp.int32, (nrows, LANES), 1)
            tri = [col_id + j * LANES <= row_id for j in range(tk // LANES)]
            for half in range(2):
                update(half * tq + lo, nrows, k_ref[start + lo:start + lo + tk, :],
                       vx_ref[start + lo:start + lo + tk, :], tri)

    for i in range(k_ref.shape[0] // tq):
        pl.when(qi == i)(functools.partial(query_tile, i))

    lam = (jnp.exp(jnp.sum(lq1_ref[...] * lk1_ref[...], axis=1, keepdims=True))
           - jnp.exp(jnp.sum(lq2_ref[...] * lk2_ref[...], axis=1, keepdims=True)) + lambda_init)
    acc = acc_ref[...]
    o = acc[:, 0:hw] / acc[:, hw:2 * hw]
    o = o[0:tq] - lam * o[tq:2 * tq]
    o_ref[...] = (_rmsnorm(o, subln_ref[...]) * (1.0 - lambda_init)).astype(o_ref.dtype)


def _attention(q, k, v, k_meta, v_meta, lq1, lk1, lq2, lk2, subln, *, seq, tq, tk, lambda_init):
    n, d = q.shape
    nq = seq // tq
    hw = HEAD_WIDTH
    lam_spec = _resident((1, HEAD_DIM))
    return pl.pallas_call(
        functools.partial(_attn_kernel, tq=tq, tk=tk, lambda_init=lambda_init),
        out_shape=jax.ShapeDtypeStruct((n, d), BF16),
        grid=(n // seq, N_HEADS, nq),
        in_specs=[
            pl.BlockSpec((tq, hw), lambda b, h, i: (b * nq + i, h)),
            pl.BlockSpec((seq, hw), lambda b, h, i: (b, h)),
            pl.BlockSpec((seq, hw), lambda b, h, i: (b, h)),
            pl.BlockSpec((N_META, hw), lambda b, h, i: (0, h)),
            pl.BlockSpec((N_META, hw), lambda b, h, i: (0, h)),
            lam_spec, lam_spec, lam_spec, lam_spec, _resident((1, hw)),
        ],
        out_specs=pl.BlockSpec((tq, hw), lambda b, h, i: (b * nq + i, h)),
        scratch_shapes=[
            pltpu.VMEM((2 * tq, hw), BF16),
            pltpu.VMEM((seq, 2 * hw), BF16),
            pltpu.VMEM((LANES, hw), BF16),
            pltpu.VMEM((LANES, 2 * hw), BF16),
            pltpu.VMEM((2 * tq, LANES), F32),
            pltpu.VMEM((2 * tq, 2 * hw), F32),
        ],
        compiler_params=_params(("parallel", "parallel", "arbitrary")),
        name="diff_attn",
    )(q, k, v, k_meta, v_meta, lq1, lk1, lq2, lk2, subln)


def kernel(x, meta_tokens, ffn_norm_pre, ffn_norm_post, ffn_w1, ffn_w3, ffn_w2, mix_norm_pre,
           mix_norm_post, pool_w, pool_b, pool_scale, attn_w_qkv, attn_w_o, attn_lambda_q1,
           attn_lambda_k1, attn_lambda_q2, attn_lambda_k2, attn_subln):
    bsz, seq, d = x.shape
    depth = ffn_w1.shape[0]
    assert d == D_MODEL and meta_tokens.shape == (N_META, d) and depth == 2
    tm = 512
    row = lambda a: a.reshape(1, -1)

    h = x.reshape(bsz * seq, d)
    hm = meta_tokens.astype(x.dtype)

    w1, w3, w2 = ffn_w1.astype(BF16), ffn_w3.astype(BF16), ffn_w2.astype(BF16)

    def ffn_main(h, i, s, **mixer):
        return _ffn(h, row(ffn_norm_pre[i, s]), w1, w3, w2, row(ffn_norm_post[i, s]), layer=i, slot=s,
                    tm=2 * tm, sub=tm, seq=seq, **mixer)

    def ffn_meta(hm, i, s):
        return _ffn(hm, row(ffn_norm_pre[i, s]), w1, w3, w2, row(ffn_norm_post[i, s]), layer=i, slot=s,
                    tm=N_META, sub=N_META)

    h, hm = ffn_main(h, 0, 0), ffn_meta(hm, 0, 0)
    pool_args = (row(mix_norm_pre[0]), pool_w[0].astype(BF16), row(pool_b[0]), row(pool_scale[0]),
                 row(mix_norm_post[0]))
    h = ffn_main(h, 0, 1, mixer="pool", mixer_args=(hm,) + pool_args)
    hm = ffn_meta(_pool_meta(hm, *pool_args), 0, 1)

    h, hm = ffn_main(h, 1, 0), ffn_meta(hm, 1, 0)
    g_pre = row(mix_norm_pre[1])
    w_qkv = attn_w_qkv[0].astype(BF16)
    q, k, v = _qkv(h, g_pre, w_qkv, tm=tm)
    _, k_meta, v_meta = _qkv(hm, g_pre, w_qkv, tm=N_META)
    o = _attention(q, k, v, k_meta, v_meta, row(attn_lambda_q1[0]), row(attn_lambda_k1[0]),
                   row(attn_lambda_q2[0]), row(attn_lambda_k2[0]), row(attn_subln[0]),
                   seq=seq, tq=1024, tk=MXU_TILE, lambda_init=_lambda_init(1))
    h = ffn_main(h, 1, 1, mixer="oproj",
                 mixer_args=(o, attn_w_o[0].astype(BF16), row(mix_norm_post[1])))
    return h.reshape(bsz, seq, d)
```

```python
import functools
import math

import jax
import jax.numpy as jnp
from jax import lax
from jax.experimental import pallas as pl
from jax.experimental.pallas import tpu as pltpu

D_MODEL = 1024
N_META = 16
POOL_WINDOWS = (2, 4, 8, 16)
POOL_GROUP = D_MODEL // len(POOL_WINDOWS)
HEAD_DIM = 64
HEAD_WIDTH = 2 * HEAD_DIM
N_HEADS = D_MODEL // HEAD_WIDTH
EPS = 1e-6
HALO = 16
MASK_VALUE = float(jnp.finfo(jnp.float32).min)
LOG2_E = math.log2(math.e)

LANES = 128
MXU_TILE = 256
VMEM_LIMIT_BYTES = 56 * 1024 * 1024

F32 = jnp.float32
BF16 = jnp.bfloat16


def _lambda_init(layer_idx):
    return 0.8 - 0.6 * math.exp(-0.3 * layer_idx)


def _rmsnorm(x, g):
    return x * lax.rsqrt(jnp.mean(x * x, axis=-1, keepdims=True) + EPS) * g


def _resident(shape):
    return pl.BlockSpec(shape, lambda *_: (0,) * len(shape), pipeline_mode=pl.Buffered(1))


def _params(semantics):
    return pltpu.CompilerParams(dimension_semantics=semantics, vmem_limit_bytes=VMEM_LIMIT_BYTES)


N_MIXER_REFS = {None: 0, "pool": 7, "oproj": 3}


def _ffn_kernel(*refs, f_chunks, sub, mixer):
    mixer_refs = refs[:N_MIXER_REFS[mixer]]
    h_ref, gpre_ref, w1_ref, w3_ref, w2_ref, gpost_ref, o_ref, acc_ref = refs[N_MIXER_REFS[mixer]:]
    groups = [slice(r0, r0 + sub) for r0 in range(0, h_ref.shape[0], sub)]
    for rows in groups if mixer is not None else ():
        h = h_ref[rows, :]
        if mixer == "pool":
            prev_ref, meta_ref, mgpre_ref, pw_ref, pb_ref, pscale_ref, mgpost_ref = mixer_refs
            if rows.start == 0:
                halo = jnp.where(pl.program_id(1) == 0, meta_ref[...], prev_ref[...])
            else:
                halo = h_ref[rows.start - HALO:rows.start, :]
            o_ref[rows, :] = _pool_mix(h, halo, mgpre_ref[...], pw_ref, pb_ref, pscale_ref,
                                       mgpost_ref[...])
        elif mixer == "oproj":
            attn_ref, wo_ref, mgpost_ref = mixer_refs
            m = jnp.dot(attn_ref[rows, :], wo_ref[...], preferred_element_type=F32)
            o_ref[rows, :] = h + _rmsnorm(m, mgpost_ref[...])
    for rows in groups:
        h = h_ref[rows, :] if mixer is None else o_ref[rows, :]
        r = lax.rsqrt(jnp.mean(h * h, axis=-1, keepdims=True) + EPS)
        hg = (h * gpre_ref[...]).astype(BF16)
        for idx, (lo, size) in enumerate(f_chunks):
            a = jnp.dot(hg, w1_ref[:, lo:lo + size], preferred_element_type=F32) * r
            b = jnp.dot(hg, w3_ref[:, lo:lo + size], preferred_element_type=F32) * r
            g = (jax.nn.silu(a) * b).astype(BF16)
            part = jnp.dot(g, w2_ref[lo:lo + size, :], preferred_element_type=F32)
            if idx == 0:
                acc_ref[rows, :] = part
            else:
                acc_ref[rows, :] += part
        o_ref[rows, :] = h + 0.5 * _rmsnorm(acc_ref[rows, :], gpost_ref[...])


def _ffn(h, gpre, w1, w3, w2, gpost, *, layer, slot, tm, sub, seq=None, mixer=None, mixer_args=()):
    n, d = h.shape
    f = w1.shape[-1]
    seq = n if seq is None else seq
    tiles = seq // tm
    f_chunks = tuple((lo, min(1024, f - lo)) for lo in range(0, f, 1024))
    row = pl.BlockSpec((tm, d), lambda b, i: (b * tiles + i, 0))
    vec = _resident((1, d))

    def weight(rows, cols):
        return pl.BlockSpec((None, None, rows, cols), lambda b, i: (layer, slot, 0, 0),
                            pipeline_mode=pl.Buffered(1))

    if mixer == "pool":
        prev = pl.BlockSpec(
            (HALO, d), lambda b, i: (jnp.maximum(b * (seq // HALO) + i * (tm // HALO) - 1, 0), 0))
        mixer_ops = (h,) + tuple(mixer_args)
        mixer_specs = [prev, _resident((N_META, d)), vec, _resident(mixer_args[2].shape), vec, vec, vec]
    elif mixer == "oproj":
        mixer_ops = tuple(mixer_args)
        mixer_specs = [row, _resident((d, d)), vec]
    else:
        mixer_ops, mixer_specs = (), []
    assert len(mixer_specs) == N_MIXER_REFS[mixer]

    return pl.pallas_call(
        functools.partial(_ffn_kernel, f_chunks=f_chunks, sub=sub, mixer=mixer),
        out_shape=jax.ShapeDtypeStruct((n, d), F32),
        grid=(n // seq, tiles),
        in_specs=mixer_specs + [row, vec, weight(d, f), weight(d, f), weight(f, d), vec],
        out_specs=row,
        scratch_shapes=[pltpu.VMEM((tm, d), F32)],
        compiler_params=_params(("parallel", "parallel")),
        name="ffn" if mixer is None else "ffn_" + mixer,
    )(*mixer_ops, h, gpre, w1, w3, w2, gpost)


def _pool_mix(h, halo, gpre, w_ref, b_ref, scale_ref, gpost):
    rows = h.shape[0]
    is_meta = halo is None
    hn = _rmsnorm(h, gpre)
    halo_n = jnp.zeros((HALO, D_MODEL), F32) if is_meta else _rmsnorm(halo, gpre)
    ext = jnp.concatenate([halo_n, hn], axis=0)
    ys = []
    for gi, win in enumerate(POOL_WINDOWS):
        cols = slice(gi * POOL_GROUP, (gi + 1) * POOL_GROUP)
        acc = ext[:, cols]
        step = 1
        while step < win:
            acc = acc + pltpu.roll(acc, step, axis=0)
            step *= 2
        wsum = acc[HALO:]
        if is_meta:
            pos = lax.broadcasted_iota(jnp.int32, (rows, POOL_GROUP), 0)
            pooled = wsum / jnp.minimum(pos + 1, win).astype(F32)
        else:
            pooled = wsum * (1.0 / win)
        diff = (pooled - hn[:, cols]).astype(BF16)
        y = jnp.dot(diff, w_ref[gi], preferred_element_type=F32) + b_ref[:, cols]
        ys.append(y * scale_ref[:, cols])
    m = jnp.concatenate(ys, axis=1)
    return h + _rmsnorm(m, gpost)


def _pool_meta_kernel(h_ref, gpre_ref, w_ref, b_ref, scale_ref, gpost_ref, o_ref):
    o_ref[...] = _pool_mix(h_ref[...], None, gpre_ref[...], w_ref, b_ref, scale_ref, gpost_ref[...])


def _pool_meta(h, gpre, w, b, scale, gpost):
    n, d = h.shape
    vec = _resident((1, d))
    return pl.pallas_call(
        _pool_meta_kernel,
        out_shape=jax.ShapeDtypeStruct((n, d), F32),
        grid=(1,),
        in_specs=[_resident((n, d)), vec, _resident(w.shape), vec, vec, vec],
        out_specs=pl.BlockSpec((n, d), lambda i: (0, 0)),
        compiler_params=_params(("arbitrary",)),
        name="pool_meta",
    )(h, gpre, w, b, scale, gpost)


def _qkv_kernel(h_ref, g_ref, w_ref, q_ref, k_ref, v_ref):
    hn = _rmsnorm(h_ref[...], g_ref[...]).astype(BF16)
    d = D_MODEL
    q = jnp.dot(hn, w_ref[:, 0:d], preferred_element_type=F32)
    q_ref[...] = (q * (HEAD_DIM ** -0.5 * LOG2_E)).astype(BF16)
    k_ref[...] = jnp.dot(hn, w_ref[:, d:2 * d], preferred_element_type=F32).astype(BF16)
    v_ref[...] = jnp.dot(hn, w_ref[:, 2 * d:3 * d], preferred_element_type=F32).astype(BF16)


def _qkv(h, g, w, *, tm):
    n, d = h.shape
    row = pl.BlockSpec((tm, d), lambda i: (i, 0))
    out = jax.ShapeDtypeStruct((n, d), BF16)
    return pl.pallas_call(
        _qkv_kernel,
        out_shape=(out, out, out),
        grid=(n // tm,),
        in_specs=[row, _resident((1, d)), _resident((d, 3 * d))],
        out_specs=(row, row, row),
        compiler_params=_params(("parallel",)),
        name="qkv",
    )(h, g, w)


def _attn_kernel(q_ref, k_ref, v_ref, km_ref, vm_ref, lq1_ref, lk1_ref, lq2_ref, lk2_ref, subln_ref,
                 o_ref, qq_ref, vx_ref, kmx_ref, vmx_ref, m_ref, acc_ref, *, tq, tk, lambda_init):
    qi = pl.program_id(2)
    nt = (((1,), (1,)), ((), ()))
    hw = HEAD_WIDTH

    @pl.when(qi == 0)
    def _stage_keys_values():
        vx_ref[:, 0:hw] = v_ref[...]
        vx_ref[:, hw:2 * hw] = jnp.ones((v_ref.shape[0], hw), BF16)
        kmx_ref[...] = jnp.zeros(kmx_ref.shape, BF16)
        kmx_ref[0:N_META, :] = km_ref[...]
        vmx_ref[...] = jnp.zeros(vmx_ref.shape, BF16)
        vmx_ref[0:N_META, 0:hw] = vm_ref[...]
        vmx_ref[0:N_META, hw:2 * hw] = jnp.ones((N_META, hw), BF16)

    q = q_ref[...]
    lane = lax.broadcasted_iota(jnp.int32, q.shape, 1)
    zero = jnp.zeros_like(q)
    qq_ref[0:tq, :] = jnp.where(lane < HEAD_DIM, q, zero)
    qq_ref[tq:2 * tq, :] = jnp.where(lane >= HEAD_DIM, q, zero)

    def update(r0, nrows, k, vx, masks, first=False):
        rows = slice(r0, r0 + nrows)
        s = lax.dot_general(qq_ref[rows, :], k, nt, preferred_element_type=F32)
        blocks = []
        for j, keep in enumerate(masks):
            blk = s[:, j * LANES:(j + 1) * LANES]
            blocks.append(blk if keep is None else jnp.where(keep, blk, MASK_VALUE))
        m_tile = jnp.max(functools.reduce(jnp.maximum, blocks), axis=1, keepdims=True)
        if first:
            m_new = jnp.broadcast_to(m_tile, (nrows, LANES))
        else:
            m_prev = m_ref[rows, :]
            m_new = jnp.maximum(m_prev, m_tile)
            alpha = jnp.exp2(m_prev - m_new)
        p = jnp.concatenate([jnp.exp2(blk - m_new) for blk in blocks], axis=1).astype(BF16)
        pv = jnp.dot(p, vx, preferred_element_type=F32)
        if first:
            acc_ref[rows, :] = pv
        else:
            acc_ref[rows, :] = jnp.concatenate([alpha, alpha], axis=1) * acc_ref[rows, :] + pv
        m_ref[rows, :] = m_new

    def query_tile(i):
        start = i * tq
        meta_cols = lax.broadcasted_iota(jnp.int32, (tq, LANES), 1)
        for half in range(2):
            update(half * tq, tq, kmx_ref[...], vmx_ref[...], [meta_cols < N_META], first=True)
        for lo in range(0, start, 2 * tk):
            for r0 in range(0, 2 * tq, tk):
                update(r0, tk, k_ref[lo:lo + 2 * tk, :], vx_ref[lo:lo + 2 * tk, :],
                       [None] * (2 * tk // LANES))
        for lo in range(0, tq, tk):
            nrows = tq - lo
            row_id = lax.broadcasted_iota(jnp.int32, (nrows, LANES), 0)
            col_id = lax.broadcasted_iota(jnp.int32, (nrows, LANES), 1)
            tri = [col_id + j * LANES <= row_id for j in range(tk // LANES)]
            for half in range(2):
                update(half * tq + lo, nrows, k_ref[start + lo:start + lo + tk, :],
                       vx_ref[start + lo:start + lo + tk, :], tri)

    for i in range(k_ref.shape[0] // tq):
        pl.when(qi == i)(functools.partial(query_tile, i))

    lam = (jnp.exp(jnp.sum(lq1_ref[...] * lk1_ref[...], axis=1, keepdims=True))
           - jnp.exp(jnp.sum(lq2_ref[...] * lk2_ref[...], axis=1, keepdims=True)) + lambda_init)
    acc = acc_ref[...]
    o = acc[:, 0:hw] / acc[:, hw:2 * hw]
    o = o[0:tq] - lam * o[tq:2 * tq]
    o_ref[...] = (_rmsnorm(o, subln_ref[...]) * (1.0 - lambda_init)).astype(o_ref.dtype)


def _attention(q, k, v, k_meta, v_meta, lq1, lk1, lq2, lk2, subln, *, seq, tq, tk, lambda_init):
    n, d = q.shape
    nq = seq // tq
    hw = HEAD_WIDTH
    lam_spec = _resident((1, HEAD_DIM))
    return pl.pallas_call(
        functools.partial(_attn_kernel, tq=tq, tk=tk, lambda_init=lambda_init),
        out_shape=jax.ShapeDtypeStruct((n, d), BF16),
        grid=(n // seq, N_HEADS, nq),
        in_specs=[
            pl.BlockSpec((tq, hw), lambda b, h, i: (b * nq + i, h)),
            pl.BlockSpec((seq, hw), lambda b, h, i: (b, h)),
            pl.BlockSpec((seq, hw), lambda b, h, i: (b, h)),
            pl.BlockSpec((N_META, hw), lambda b, h, i: (0, h)),
            pl.BlockSpec((N_META, hw), lambda b, h, i: (0, h)),
            lam_spec, lam_spec, lam_spec, lam_spec, _resident((1, hw)),
        ],
        out_specs=pl.BlockSpec((tq, hw), lambda b, h, i: (b * nq + i, h)),
        scratch_shapes=[
            pltpu.VMEM((2 * tq, hw), BF16),
            pltpu.VMEM((seq, 2 * hw), BF16),
            pltpu.VMEM((LANES, hw), BF16),
            pltpu.VMEM((LANES, 2 * hw), BF16),
            pltpu.VMEM((2 * tq, LANES), F32),
            pltpu.VMEM((2 * tq, 2 * hw), F32),
        ],
        compiler_params=_params(("parallel", "parallel", "arbitrary")),
        name="diff_attn",
    )(q, k, v, k_meta, v_meta, lq1, lk1, lq2, lk2, subln)


def kernel(x, meta_tokens, ffn_norm_pre, ffn_norm_post, ffn_w1, ffn_w3, ffn_w2, mix_norm_pre,
           mix_norm_post, pool_w, pool_b, pool_scale, attn_w_qkv, attn_w_o, attn_lambda_q1,
           attn_lambda_k1, attn_lambda_q2, attn_lambda_k2, attn_subln):
    bsz, seq, d = x.shape
    depth = ffn_w1.shape[0]
    assert d == D_MODEL and meta_tokens.shape == (N_META, d) and depth == 2
    tm = 512
    row = lambda a: a.reshape(1, -1)

    h = x.reshape(bsz * seq, d)
    hm = meta_tokens.astype(x.dtype)

    w1, w3, w2 = ffn_w1.astype(BF16), ffn_w3.astype(BF16), ffn_w2.astype(BF16)

    def ffn_main(h, i, s, **mixer):
        return _ffn(h, row(ffn_norm_pre[i, s]), w1, w3, w2, row(ffn_norm_post[i, s]), layer=i, slot=s,
                    tm=2 * tm, sub=tm, seq=seq, **mixer)

    def ffn_meta(hm, i, s):
        return _ffn(hm, row(ffn_norm_pre[i, s]), w1, w3, w2, row(ffn_norm_post[i, s]), layer=i, slot=s,
                    tm=N_META, sub=N_META)

    h, hm = ffn_main(h, 0, 0), ffn_meta(hm, 0, 0)
    pool_args = (row(mix_norm_pre[0]), pool_w[0].astype(BF16), row(pool_b[0]), row(pool_scale[0]),
                 row(mix_norm_post[0]))
    h = ffn_main(h, 0, 1, mixer="pool", mixer_args=(hm,) + pool_args)
    hm = ffn_meta(_pool_meta(hm, *pool_args), 0, 1)

    h, hm = ffn_main(h, 1, 0), ffn_meta(hm, 1, 0)
    g_pre = row(mix_norm_pre[1])
    w_qkv = attn_w_qkv[0].astype(BF16)
    q, k, v = _qkv(h, g_pre, w_qkv, tm=tm)
    _, k_meta, v_meta = _qkv(hm, g_pre, w_qkv, tm=N_META)
    o = _attention(q, k, v, k_meta, v_meta, row(attn_lambda_q1[0]), row(attn_lambda_k1[0]),
                   row(attn_lambda_q2[0]), row(attn_lambda_k2[0]), row(attn_subln[0]),
                   seq=seq, tq=1024, tk=MXU_TILE, lambda_init=_lambda_init(1))
    h = ffn_main(h, 1, 1, mixer="oproj",
                 mixer_args=(o, attn_w_o[0].astype(BF16), row(mix_norm_post[1])))
    return h.reshape(bsz, seq, d)
```

```python
import functools
import math

import jax
import jax.numpy as jnp
from jax import lax
from jax.experimental import pallas as pl
from jax.experimental.pallas import tpu as pltpu

D_MODEL = 1024
N_META = 16
POOL_WINDOWS = (2, 4, 8, 16)
POOL_GROUP = D_MODEL // len(POOL_WINDOWS)
HEAD_DIM = 64
HEAD_WIDTH = 2 * HEAD_DIM
N_HEADS = D_MODEL // HEAD_WIDTH
EPS = 1e-6
HALO = 16
MASK_VALUE = float(jnp.finfo(jnp.float32).min)
LOG2_E = math.log2(math.e)

LANES = 128
MXU_TILE = 256
VMEM_LIMIT_BYTES = 56 * 1024 * 1024

F32 = jnp.float32
BF16 = jnp.bfloat16


def _lambda_init(layer_idx):
    return 0.8 - 0.6 * math.exp(-0.3 * layer_idx)


def _rmsnorm(x, g):
    return x * lax.rsqrt(jnp.mean(x * x, axis=-1, keepdims=True) + EPS) * g


def _resident(shape):
    return pl.BlockSpec(shape, lambda *_: (0,) * len(shape), pipeline_mode=pl.Buffered(1))


def _params(semantics):
    return pltpu.CompilerParams(dimension_semantics=semantics, vmem_limit_bytes=VMEM_LIMIT_BYTES)


N_MIXER_REFS = {None: 0, "pool": 7, "oproj": 3}


def _ffn_kernel(*refs, f_chunks, sub, mixer):
    mixer_refs = refs[:N_MIXER_REFS[mixer]]
    h_ref, gpre_ref, w1_ref, w3_ref, w2_ref, gpost_ref, o_ref, acc_ref = refs[N_MIXER_REFS[mixer]:]
    groups = [slice(r0, r0 + sub) for r0 in range(0, h_ref.shape[0], sub)]
    for rows in groups if mixer is not None else ():
        h = h_ref[rows, :]
        if mixer == "pool":
            prev_ref, meta_ref, mgpre_ref, pw_ref, pb_ref, pscale_ref, mgpost_ref = mixer_refs
            if rows.start == 0:
                halo = jnp.where(pl.program_id(1) == 0, meta_ref[...], prev_ref[...])
            else:
                halo = h_ref[rows.start - HALO:rows.start, :]
            o_ref[rows, :] = _pool_mix(h, halo, mgpre_ref[...], pw_ref, pb_ref, pscale_ref,
                                       mgpost_ref[...])
        elif mixer == "oproj":
            attn_ref, wo_ref, mgpost_ref = mixer_refs
            m = jnp.dot(attn_ref[rows, :], wo_ref[...], preferred_element_type=F32)
            o_ref[rows, :] = h + _rmsnorm(m, mgpost_ref[...])
    for rows in groups:
        h = h_ref[rows, :] if mixer is None else o_ref[rows, :]
        r = lax.rsqrt(jnp.mean(h * h, axis=-1, keepdims=True) + EPS)
        hg = (h * gpre_ref[...]).astype(BF16)
        for idx, (lo, size) in enumerate(f_chunks):
            a = jnp.dot(hg, w1_ref[:, lo:lo + size], preferred_element_type=F32) * r
            b = jnp.dot(hg, w3_ref[:, lo:lo + size], preferred_element_type=F32) * r
            g = (jax.nn.silu(a) * b).astype(BF16)
            part = jnp.dot(g, w2_ref[lo:lo + size, :], preferred_element_type=F32)
            if idx == 0:
                acc_ref[rows, :] = part
            else:
                acc_ref[rows, :] += part
        o_ref[rows, :] = h + 0.5 * _rmsnorm(acc_ref[rows, :], gpost_ref[...])


def _ffn(h, gpre, w1, w3, w2, gpost, *, layer, slot, tm, sub, seq=None, mixer=None, mixer_args=()):
    n, d = h.shape
    f = w1.shape[-1]
    seq = n if seq is None else seq
    tiles = seq // tm
    f_chunks = tuple((lo, min(1024, f - lo)) for lo in range(0, f, 1024))
    row = pl.BlockSpec((tm, d), lambda b, i: (b * tiles + i, 0))
    vec = _resident((1, d))

    def weight(rows, cols):
        return pl.BlockSpec((None, None, rows, cols), lambda b, i: (layer, slot, 0, 0),
                            pipeline_mode=pl.Buffered(1))

    if mixer == "pool":
        prev = pl.BlockSpec(
            (HALO, d), lambda b, i: (jnp.maximum(b * (seq // HALO) + i * (tm // HALO) - 1, 0), 0))
        mixer_ops = (h,) + tuple(mixer_args)
        mixer_specs = [prev, _resident((N_META, d)), vec, _resident(mixer_args[2].shape), vec, vec, vec]
    elif mixer == "oproj":
        mixer_ops = tuple(mixer_args)
        mixer_specs = [row, _resident((d, d)), vec]
    else:
        mixer_ops, mixer_specs = (), []
    assert len(mixer_specs) == N_MIXER_REFS[mixer]

    return pl.pallas_call(
        functools.partial(_ffn_kernel, f_chunks=f_chunks, sub=sub, mixer=mixer),
        out_shape=jax.ShapeDtypeStruct((n, d), F32),
        grid=(n // seq, tiles),
        in_specs=mixer_specs + [row, vec, weight(d, f), weight(d, f), weight(f, d), vec],
        out_specs=row,
        scratch_shapes=[pltpu.VMEM((tm, d), F32)],
        compiler_params=_params(("parallel", "parallel")),
        name="ffn" if mixer is None else "ffn_" + mixer,
    )(*mixer_ops, h, gpre, w1, w3, w2, gpost)


def _pool_mix(h, halo, gpre, w_ref, b_ref, scale_ref, gpost):
    rows = h.shape[0]
    is_meta = halo is None
    hn = _rmsnorm(h, gpre)
    halo_n = jnp.zeros((HALO, D_MODEL), F32) if is_meta else _rmsnorm(halo, gpre)
    ext = jnp.concatenate([halo_n, hn], axis=0)
    ys = []
    for gi, win in enumerate(POOL_WINDOWS):
        cols = slice(gi * POOL_GROUP, (gi + 1) * POOL_GROUP)
        acc = ext[:, cols]
        step = 1
        while step < win:
            acc = acc + pltpu.roll(acc, step, axis=0)
            step *= 2
        wsum = acc[HALO:]
        if is_meta:
            pos = lax.broadcasted_iota(jnp.int32, (rows, POOL_GROUP), 0)
            pooled = wsum / jnp.minimum(pos + 1, win).astype(F32)
        else:
            pooled = wsum * (1.0 / win)
        diff = (pooled - hn[:, cols]).astype(BF16)
        y = jnp.dot(diff, w_ref[gi], preferred_element_type=F32) + b_ref[:, cols]
        ys.append(y * scale_ref[:, cols])
    m = jnp.concatenate(ys, axis=1)
    return h + _rmsnorm(m, gpost)


def _pool_meta_kernel(h_ref, gpre_ref, w_ref, b_ref, scale_ref, gpost_ref, o_ref):
    o_ref[...] = _pool_mix(h_ref[...], None, gpre_ref[...], w_ref, b_ref, scale_ref, gpost_ref[...])


def _pool_meta(h, gpre, w, b, scale, gpost):
    n, d = h.shape
    vec = _resident((1, d))
    return pl.pallas_call(
        _pool_meta_kernel,
        out_shape=jax.ShapeDtypeStruct((n, d), F32),
        grid=(1,),
        in_specs=[_resident((n, d)), vec, _resident(w.shape), vec, vec, vec],
        out_specs=pl.BlockSpec((n, d), lambda i: (0, 0)),
        compiler_params=_params(("arbitrary",)),
        name="pool_meta",
    )(h, gpre, w, b, scale, gpost)


def _qkv_kernel(h_ref, g_ref, w_ref, q_ref, k_ref, v_ref):
    hn = _rmsnorm(h_ref[...], g_ref[...]).astype(BF16)
    d = D_MODEL
    q = jnp.dot(hn, w_ref[:, 0:d], preferred_element_type=F32)
    q_ref[...] = (q * (HEAD_DIM ** -0.5 * LOG2_E)).astype(BF16)
    k_ref[...] = jnp.dot(hn, w_ref[:, d:2 * d], preferred_element_type=F32).astype(BF16)
    v_ref[...] = jnp.dot(hn, w_ref[:, 2 * d:3 * d], preferred_element_type=F32).astype(BF16)


def _qkv(h, g, w, *, tm):
    n, d = h.shape
    row = pl.BlockSpec((tm, d), lambda i: (i, 0))
    out = jax.ShapeDtypeStruct((n, d), BF16)
    return pl.pallas_call(
        _qkv_kernel,
        out_shape=(out, out, out),
        grid=(n // tm,),
        in_specs=[row, _resident((1, d)), _resident((d, 3 * d))],
        out_specs=(row, row, row),
        compiler_params=_params(("parallel",)),
        name="qkv",
    )(h, g, w)


def _attn_kernel(q_ref, k_ref, v_ref, km_ref, vm_ref, lq1_ref, lk1_ref, lq2_ref, lk2_ref, subln_ref,
                 o_ref, qq_ref, vx_ref, kmx_ref, vmx_ref, m_ref, acc_ref, *, tq, tk, lambda_init):
    qi = pl.program_id(2)
    nt = (((1,), (1,)), ((), ()))
    hw = HEAD_WIDTH

    @pl.when(qi == 0)
    def _stage_keys_values():
        vx_ref[:, 0:hw] = v_ref[...]
        vx_ref[:, hw:2 * hw] = jnp.ones((v_ref.shape[0], hw), BF16)
        kmx_ref[...] = jnp.zeros(kmx_ref.shape, BF16)
        kmx_ref[0:N_META, :] = km_ref[...]
        vmx_ref[...] = jnp.zeros(vmx_ref.shape, BF16)
        vmx_ref[0:N_META, 0:hw] = vm_ref[...]
        vmx_ref[0:N_META, hw:2 * hw] = jnp.ones((N_META, hw), BF16)

    q = q_ref[...]
    lane = lax.broadcasted_iota(jnp.int32, q.shape, 1)
    zero = jnp.zeros_like(q)
    qq_ref[0:tq, :] = jnp.where(lane < HEAD_DIM, q, zero)
    qq_ref[tq:2 * tq, :] = jnp.where(lane >= HEAD_DIM, q, zero)

    def scores(item):
        r0, nrows, keys_ref, _, k0, n, masks, _ = item
        s = lax.dot_general(qq_ref[r0:r0 + nrows, :], keys_ref[k0:k0 + n, :], nt,
                            preferred_element_type=F32)
        blocks = []
        for j, keep in enumerate(masks):
            blk = s[:, j * LANES:(j + 1) * LANES]
            blocks.append(blk if keep is None else jnp.where(keep, blk, MASK_VALUE))
        m_tile = jnp.max(functools.reduce(jnp.maximum, blocks), axis=1, keepdims=True)
        return blocks, m_tile

    def accumulate(item, blocks, m_tile):
        r0, nrows, _, values_ref, k0, n, _, first = item
        vx = values_ref[k0:k0 + n, :]
        rows = slice(r0, r0 + nrows)
        if first:
            m_new = jnp.broadcast_to(m_tile, (nrows, LANES))
        else:
            m_prev = m_ref[rows, :]
            m_new = jnp.maximum(m_prev, m_tile)
            alpha = jnp.exp2(m_prev - m_new)
        p = jnp.concatenate([jnp.exp2(blk - m_new) for blk in blocks], axis=1).astype(BF16)
        pv = jnp.dot(p, vx, preferred_element_type=F32)
        if first:
            acc_ref[rows, :] = pv
        else:
            acc_ref[rows, :] = jnp.concatenate([alpha, alpha], axis=1) * acc_ref[rows, :] + pv
        m_ref[rows, :] = m_new

    def query_tile(i):
        start = i * tq
        meta_cols = lax.broadcasted_iota(jnp.int32, (tq, LANES), 1)
        items = [(half * tq, tq, kmx_ref, vmx_ref, 0, LANES, [meta_cols < N_META], True)
                 for half in range(2)]
        for lo in range(0, start, 2 * tk):
            for r0 in range(0, 2 * tq, tk):
                items.append((r0, tk, k_ref, vx_ref, lo, 2 * tk, [None] * (2 * tk // LANES), False))
        for lo in range(0, tq, tk):
            nrows = tq - lo
            row_id = lax.broadcasted_iota(jnp.int32, (nrows, LANES), 0)
            col_id = lax.broadcasted_iota(jnp.int32, (nrows, LANES), 1)
            tri = [col_id + j * LANES <= row_id for j in range(tk // LANES)]
            for half in range(2):
                items.append((half * tq + lo, nrows, k_ref, vx_ref, start + lo, tk, tri, False))
        pending = None
        for item in items:
            staged = scores(item)
            if pending is not None:
                accumulate(*pending)
            pending = (item,) + staged
        accumulate(*pending)

    for i in range(k_ref.shape[0] // tq):
        pl.when(qi == i)(functools.partial(query_tile, i))

    lam = (jnp.exp(jnp.sum(lq1_ref[...] * lk1_ref[...], axis=1, keepdims=True))
           - jnp.exp(jnp.sum(lq2_ref[...] * lk2_ref[...], axis=1, keepdims=True)) + lambda_init)
    acc = acc_ref[...]
    o = acc[:, 0:hw] / acc[:, hw:2 * hw]
    o = o[0:tq] - lam * o[tq:2 * tq]
    o_ref[...] = (_rmsnorm(o, subln_ref[...]) * (1.0 - lambda_init)).astype(o_ref.dtype)


def _attention(q, k, v, k_meta, v_meta, lq1, lk1, lq2, lk2, subln, *, seq, tq, tk, lambda_init):
    n, d = q.shape
    nq = seq // tq
    hw = HEAD_WIDTH
    lam_spec = _resident((1, HEAD_DIM))
    return pl.pallas_call(
        functools.partial(_attn_kernel, tq=tq, tk=tk, lambda_init=lambda_init),
        out_shape=jax.ShapeDtypeStruct((n, d), BF16),
        grid=(n // seq, N_HEADS, nq),
        in_specs=[
            pl.BlockSpec((tq, hw), lambda b, h, i: (b * nq + i, h)),
            pl.BlockSpec((seq, hw), lambda b, h, i: (b, h)),
            pl.BlockSpec((seq, hw), lambda b, h, i: (b, h)),
            pl.BlockSpec((N_META, hw), lambda b, h, i: (0, h)),
            pl.BlockSpec((N_META, hw), lambda b, h, i: (0, h)),
            lam_spec, lam_spec, lam_spec, lam_spec, _resident((1, hw)),
        ],
        out_specs=pl.BlockSpec((tq, hw), lambda b, h, i: (b * nq + i, h)),
        scratch_shapes=[
            pltpu.VMEM((2 * tq, hw), BF16),
            pltpu.VMEM((seq, 2 * hw), BF16),
            pltpu.VMEM((LANES, hw), BF16),
            pltpu.VMEM((LANES, 2 * hw), BF16),
            pltpu.VMEM((2 * tq, LANES), F32),
            pltpu.VMEM((2 * tq, 2 * hw), F32),
        ],
        compiler_params=_params(("parallel", "parallel", "arbitrary")),
        name="diff_attn",
    )(q, k, v, k_meta, v_meta, lq1, lk1, lq2, lk2, subln)


def kernel(x, meta_tokens, ffn_norm_pre, ffn_norm_post, ffn_w1, ffn_w3, ffn_w2, mix_norm_pre,
           mix_norm_post, pool_w, pool_b, pool_scale, attn_w_qkv, attn_w_o, attn_lambda_q1,
           attn_lambda_k1, attn_lambda_q2, attn_lambda_k2, attn_subln):
    bsz, seq, d = x.shape
    depth = ffn_w1.shape[0]
    assert d == D_MODEL and meta_tokens.shape == (N_META, d) and depth == 2
    tm = 512
    row = lambda a: a.reshape(1, -1)

    h = x.reshape(bsz * seq, d)
    hm = meta_tokens.astype(x.dtype)

    w1, w3, w2 = ffn_w1.astype(BF16), ffn_w3.astype(BF16), ffn_w2.astype(BF16)

    def ffn_main(h, i, s, **mixer):
        return _ffn(h, row(ffn_norm_pre[i, s]), w1, w3, w2, row(ffn_norm_post[i, s]), layer=i, slot=s,
                    tm=2 * tm, sub=tm, seq=seq, **mixer)

    def ffn_meta(hm, i, s):
        return _ffn(hm, row(ffn_norm_pre[i, s]), w1, w3, w2, row(ffn_norm_post[i, s]), layer=i, slot=s,
                    tm=N_META, sub=N_META)

    h, hm = ffn_main(h, 0, 0), ffn_meta(hm, 0, 0)
    pool_args = (row(mix_norm_pre[0]), pool_w[0].astype(BF16), row(pool_b[0]), row(pool_scale[0]),
                 row(mix_norm_post[0]))
    h = ffn_main(h, 0, 1, mixer="pool", mixer_args=(hm,) + pool_args)
    hm = ffn_meta(_pool_meta(hm, *pool_args), 0, 1)

    h, hm = ffn_main(h, 1, 0), ffn_meta(hm, 1, 0)
    g_pre = row(mix_norm_pre[1])
    w_qkv = attn_w_qkv[0].astype(BF16)
    q, k, v = _qkv(h, g_pre, w_qkv, tm=tm)
    _, k_meta, v_meta = _qkv(hm, g_pre, w_qkv, tm=N_META)
    o = _attention(q, k, v, k_meta, v_meta, row(attn_lambda_q1[0]), row(attn_lambda_k1[0]),
                   row(attn_lambda_q2[0]), row(attn_lambda_k2[0]), row(attn_subln[0]),
                   seq=seq, tq=1024, tk=MXU_TILE, lambda_init=_lambda_init(1))
    h = ffn_main(h, 1, 1, mixer="oproj",
                 mixer_args=(o, attn_w_o[0].astype(BF16), row(mix_norm_post[1])))
    return h.reshape(bsz, seq, d)
```

```python
import functools
import itertools
import math

import jax
import jax.numpy as jnp
from jax import lax
from jax.experimental import pallas as pl
from jax.experimental.pallas import tpu as pltpu

D_MODEL = 1024
N_META = 16
POOL_WINDOWS = (2, 4, 8, 16)
POOL_GROUP = D_MODEL // len(POOL_WINDOWS)
HEAD_DIM = 64
HEAD_WIDTH = 2 * HEAD_DIM
N_HEADS = D_MODEL // HEAD_WIDTH
EPS = 1e-6
HALO = 16
MASK_VALUE = float(jnp.finfo(jnp.float32).min)
LOG2_E = math.log2(math.e)

LANES = 128
MXU_TILE = 256
VMEM_LIMIT_BYTES = 56 * 1024 * 1024

F32 = jnp.float32
BF16 = jnp.bfloat16


def _lambda_init(layer_idx):
    return 0.8 - 0.6 * math.exp(-0.3 * layer_idx)


def _rmsnorm(x, g):
    return x * lax.rsqrt(jnp.mean(x * x, axis=-1, keepdims=True) + EPS) * g


def _resident(shape):
    return pl.BlockSpec(shape, lambda *_: (0,) * len(shape), pipeline_mode=pl.Buffered(1))


def _params(semantics):
    return pltpu.CompilerParams(dimension_semantics=semantics, vmem_limit_bytes=VMEM_LIMIT_BYTES)


N_MIXER_REFS = {None: 0, "pool": 7, "oproj": 3}


def _ffn_kernel(*refs, f_chunks, sub, mixer):
    mixer_refs = refs[:N_MIXER_REFS[mixer]]
    h_ref, gpre_ref, w1_ref, w3_ref, w2_ref, gpost_ref, o_ref, acc_ref = refs[N_MIXER_REFS[mixer]:]
    groups = [slice(r0, r0 + sub) for r0 in range(0, h_ref.shape[0], sub)]

    def mixer_steps(rows):
        h = h_ref[rows, :]
        if mixer == "pool":
            prev_ref, meta_ref, mgpre_ref, pw_ref, pb_ref, pscale_ref, mgpost_ref = mixer_refs
            if rows.start == 0:
                halo = jnp.where(pl.program_id(1) == 0, meta_ref[...], prev_ref[...])
            else:
                halo = h_ref[rows.start - HALO:rows.start, :]
            for out in _pool_mix_steps(h, halo, mgpre_ref[...], pw_ref, pb_ref, pscale_ref,
                                       mgpost_ref[...]):
                if out is not None:
                    o_ref[rows, :] = out
                yield
        elif mixer == "oproj":
            attn_ref, wo_ref, mgpost_ref = mixer_refs
            m = jnp.dot(attn_ref[rows, :], wo_ref[...], preferred_element_type=F32)
            o_ref[rows, :] = h + _rmsnorm(m, mgpost_ref[...])
            yield

    def ffn_steps(rows):
        h = h_ref[rows, :] if mixer is None else o_ref[rows, :]
        r = lax.rsqrt(jnp.mean(h * h, axis=-1, keepdims=True) + EPS)
        hg = (h * gpre_ref[...]).astype(BF16)
        for idx, (lo, size) in enumerate(f_chunks):
            a = jnp.dot(hg, w1_ref[:, lo:lo + size], preferred_element_type=F32) * r
            b = jnp.dot(hg, w3_ref[:, lo:lo + size], preferred_element_type=F32) * r
            g = (jax.nn.silu(a) * b).astype(BF16)
            part = jnp.dot(g, w2_ref[lo:lo + size, :], preferred_element_type=F32)
            if idx == 0:
                acc_ref[rows, :] = part
            else:
                acc_ref[rows, :] += part
            yield
        o_ref[rows, :] = h + 0.5 * _rmsnorm(acc_ref[rows, :], gpost_ref[...])
        yield

    for _ in mixer_steps(groups[0]):
        pass
    for rows, nxt in zip(groups, groups[1:] + [None]):
        for _ in itertools.zip_longest(ffn_steps(rows), mixer_steps(nxt) if nxt else ()):
            pass


def _ffn(h, gpre, w1, w3, w2, gpost, *, layer, slot, tm, sub, seq=None, mixer=None, mixer_args=()):
    n, d = h.shape
    f = w1.shape[-1]
    seq = n if seq is None else seq
    tiles = seq // tm
    f_chunk = 512 if mixer == "pool" else 1024
    f_chunks = tuple((lo, min(f_chunk, f - lo)) for lo in range(0, f, f_chunk))
    row = pl.BlockSpec((tm, d), lambda b, i: (b * tiles + i, 0))
    vec = _resident((1, d))

    def weight(rows, cols):
        return pl.BlockSpec((None, None, rows, cols), lambda b, i: (layer, slot, 0, 0),
                            pipeline_mode=pl.Buffered(1))

    if mixer == "pool":
        prev = pl.BlockSpec(
            (HALO, d), lambda b, i: (jnp.maximum(b * (seq // HALO) + i * (tm // HALO) - 1, 0), 0))
        mixer_ops = (h,) + tuple(mixer_args)
        mixer_specs = [prev, _resident((N_META, d)), vec, _resident(mixer_args[2].shape), vec, vec, vec]
    elif mixer == "oproj":
        mixer_ops = tuple(mixer_args)
        mixer_specs = [row, _resident((d, d)), vec]
    else:
        mixer_ops, mixer_specs = (), []
    assert len(mixer_specs) == N_MIXER_REFS[mixer]

    return pl.pallas_call(
        functools.partial(_ffn_kernel, f_chunks=f_chunks, sub=sub, mixer=mixer),
        out_shape=jax.ShapeDtypeStruct((n, d), F32),
        grid=(n // seq, tiles),
        in_specs=mixer_specs + [row, vec, weight(d, f), weight(d, f), weight(f, d), vec],
        out_specs=row,
        scratch_shapes=[pltpu.VMEM((tm, d), F32)],
        compiler_params=_params(("parallel", "parallel")),
        name="ffn" if mixer is None else "ffn_" + mixer,
    )(*mixer_ops, h, gpre, w1, w3, w2, gpost)


def _pool_mix_steps(h, halo, gpre, w_ref, b_ref, scale_ref, gpost):
    rows = h.shape[0]
    is_meta = halo is None
    hn = _rmsnorm(h, gpre)
    halo_n = jnp.zeros((HALO, D_MODEL), F32) if is_meta else _rmsnorm(halo, gpre)
    ext = jnp.concatenate([halo_n, hn], axis=0)
    yield None
    ys = []
    for gi, win in enumerate(POOL_WINDOWS):
        cols = slice(gi * POOL_GROUP, (gi + 1) * POOL_GROUP)
        acc = ext[:, cols]
        step = 1
        while step < win:
            acc = acc + pltpu.roll(acc, step, axis=0)
            step *= 2
        wsum = acc[HALO:]
        if is_meta:
            pos = lax.broadcasted_iota(jnp.int32, (rows, POOL_GROUP), 0)
            pooled = wsum / jnp.minimum(pos + 1, win).astype(F32)
        else:
            pooled = wsum * (1.0 / win)
        diff = (pooled - hn[:, cols]).astype(BF16)
        y = jnp.dot(diff, w_ref[gi], preferred_element_type=F32) + b_ref[:, cols]
        ys.append(y * scale_ref[:, cols])
        yield None
    m = jnp.concatenate(ys, axis=1)
    yield h + _rmsnorm(m, gpost)


def _pool_meta_kernel(h_ref, gpre_ref, w_ref, b_ref, scale_ref, gpost_ref, o_ref):
    *_, o_ref[...] = _pool_mix_steps(h_ref[...], None, gpre_ref[...], w_ref, b_ref, scale_ref, gpost_ref[...])


def _pool_meta(h, gpre, w, b, scale, gpost):
    n, d = h.shape
    vec = _resident((1, d))
    return pl.pallas_call(
        _pool_meta_kernel,
        out_shape=jax.ShapeDtypeStruct((n, d), F32),
        grid=(1,),
        in_specs=[_resident((n, d)), vec, _resident(w.shape), vec, vec, vec],
        out_specs=pl.BlockSpec((n, d), lambda i: (0, 0)),
        compiler_params=_params(("arbitrary",)),
        name="pool_meta",
    )(h, gpre, w, b, scale, gpost)


def _qkv_kernel(h_ref, g_ref, w_ref, q_ref, k_ref, v_ref):
    hn = _rmsnorm(h_ref[...], g_ref[...]).astype(BF16)
    d = D_MODEL
    q = jnp.dot(hn, w_ref[:, 0:d], preferred_element_type=F32)
    q_ref[...] = (q * (HEAD_DIM ** -0.5 * LOG2_E)).astype(BF16)
    k_ref[...] = jnp.dot(hn, w_ref[:, d:2 * d], preferred_element_type=F32).astype(BF16)
    v_ref[...] = jnp.dot(hn, w_ref[:, 2 * d:3 * d], preferred_element_type=F32).astype(BF16)


def _qkv(h, g, w, *, tm):
    n, d = h.shape
    row = pl.BlockSpec((tm, d), lambda i: (i, 0))
    out = jax.ShapeDtypeStruct((n, d), BF16)
    return pl.pallas_call(
        _qkv_kernel,
        out_shape=(out, out, out),
        grid=(n // tm,),
        in_specs=[row, _resident((1, d)), _resident((d, 3 * d))],
        out_specs=(row, row, row),
        compiler_params=_params(("parallel",)),
        name="qkv",
    )(h, g, w)


def _attn_kernel(q_ref, k_ref, v_ref, km_ref, vm_ref, lq1_ref, lk1_ref, lq2_ref, lk2_ref, subln_ref,
                 o_ref, qq_ref, vx_ref, kmx_ref, vmx_ref, m_ref, acc_ref, *, tq, tk, lambda_init):
    qi = pl.program_id(2)
    nt = (((1,), (1,)), ((), ()))
    hw = HEAD_WIDTH

    @pl.when(qi == 0)
    def _stage_keys_values():
        vx_ref[:, 0:hw] = v_ref[...]
        vx_ref[:, hw:2 * hw] = jnp.ones((v_ref.shape[0], hw), BF16)
        kmx_ref[...] = jnp.zeros(kmx_ref.shape, BF16)
        kmx_ref[0:N_META, :] = km_ref[...]
        vmx_ref[...] = jnp.zeros(vmx_ref.shape, BF16)
        vmx_ref[0:N_META, 0:hw] = vm_ref[...]
        vmx_ref[0:N_META, hw:2 * hw] = jnp.ones((N_META, hw), BF16)

    q = q_ref[...]
    lane = lax.broadcasted_iota(jnp.int32, q.shape, 1)
    zero = jnp.zeros_like(q)
    qq_ref[0:tq, :] = jnp.where(lane < HEAD_DIM, q, zero)
    qq_ref[tq:2 * tq, :] = jnp.where(lane >= HEAD_DIM, q, zero)

    def scores(item):
        r0, nrows, keys_ref, _, k0, n, masks, _ = item
        s = lax.dot_general(qq_ref[r0:r0 + nrows, :], keys_ref[k0:k0 + n, :], nt,
                            preferred_element_type=F32)
        blocks = []
        for j, keep in enumerate(masks):
            blk = s[:, j * LANES:(j + 1) * LANES]
            blocks.append(blk if keep is None else jnp.where(keep, blk, MASK_VALUE))
        m_tile = jnp.max(functools.reduce(jnp.maximum, blocks), axis=1, keepdims=True)
        return blocks, m_tile

    def accumulate(item, blocks, m_tile):
        r0, nrows, _, values_ref, k0, n, _, first = item
        vx = values_ref[k0:k0 + n, :]
        rows = slice(r0, r0 + nrows)
        if first:
            m_new = jnp.broadcast_to(m_tile, (nrows, LANES))
        else:
            m_prev = m_ref[rows, :]
            m_new = jnp.maximum(m_prev, m_tile)
            alpha = jnp.exp2(m_prev - m_new)
        p = jnp.concatenate([jnp.exp2(blk - m_new) for blk in blocks], axis=1).astype(BF16)
        pv = jnp.dot(p, vx, preferred_element_type=F32)
        if first:
            acc_ref[rows, :] = pv
        else:
            acc_ref[rows, :] = jnp.concatenate([alpha, alpha], axis=1) * acc_ref[rows, :] + pv
        m_ref[rows, :] = m_new

    def query_tile(i):
        start = i * tq
        meta_cols = lax.broadcasted_iota(jnp.int32, (tq, LANES), 1)
        items = [(half * tq, tq, kmx_ref, vmx_ref, 0, LANES, [meta_cols < N_META], True)
                 for half in range(2)]
        for lo in range(0, start, 2 * tk):
            for r0 in range(0, 2 * tq, tk):
                items.append((r0, tk, k_ref, vx_ref, lo, 2 * tk, [None] * (2 * tk // LANES), False))
        for lo in range(0, tq, tk):
            nrows = tq - lo
            row_id = lax.broadcasted_iota(jnp.int32, (nrows, LANES), 0)
            col_id = lax.broadcasted_iota(jnp.int32, (nrows, LANES), 1)
            tri = [col_id + j * LANES <= row_id for j in range(tk // LANES)]
            for half in range(2):
                items.append((half * tq + lo, nrows, k_ref, vx_ref, start + lo, tk, tri, False))
        pending = None
        for item in items:
            staged = scores(item)
            if pending is not None:
                accumulate(*pending)
            pending = (item,) + staged
        accumulate(*pending)

    for i in range(k_ref.shape[0] // tq):
        pl.when(qi == i)(functools.partial(query_tile, i))

    lam = (jnp.exp(jnp.sum(lq1_ref[...] * lk1_ref[...], axis=1, keepdims=True))
           - jnp.exp(jnp.sum(lq2_ref[...] * lk2_ref[...], axis=1, keepdims=True)) + lambda_init)
    acc = acc_ref[...]
    o = acc[:, 0:hw] / acc[:, hw:2 * hw]
    o = o[0:tq] - lam * o[tq:2 * tq]
    o_ref[...] = (_rmsnorm(o, subln_ref[...]) * (1.0 - lambda_init)).astype(o_ref.dtype)


def _attention(q, k, v, k_meta, v_meta, lq1, lk1, lq2, lk2, subln, *, seq, tq, tk, lambda_init):
    n, d = q.shape
    nq = seq // tq
    hw = HEAD_WIDTH
    lam_spec = _resident((1, HEAD_DIM))
    return pl.pallas_call(
        functools.partial(_attn_kernel, tq=tq, tk=tk, lambda_init=lambda_init),
        out_shape=jax.ShapeDtypeStruct((n, d), BF16),
        grid=(n // seq, N_HEADS, nq),
        in_specs=[
            pl.BlockSpec((tq, hw), lambda b, h, i: (b * nq + i, h)),
            pl.BlockSpec((seq, hw), lambda b, h, i: (b, h)),
            pl.BlockSpec((seq, hw), lambda b, h, i: (b, h)),
            pl.BlockSpec((N_META, hw), lambda b, h, i: (0, h)),
            pl.BlockSpec((N_META, hw), lambda b, h, i: (0, h)),
            lam_spec, lam_spec, lam_spec, lam_spec, _resident((1, hw)),
        ],
        out_specs=pl.BlockSpec((tq, hw), lambda b, h, i: (b * nq + i, h)),
        scratch_shapes=[
            pltpu.VMEM((2 * tq, hw), BF16),
            pltpu.VMEM((seq, 2 * hw), BF16),
            pltpu.VMEM((LANES, hw), BF16),
            pltpu.VMEM((LANES, 2 * hw), BF16),
            pltpu.VMEM((2 * tq, LANES), F32),
            pltpu.VMEM((2 * tq, 2 * hw), F32),
        ],
        compiler_params=_params(("parallel", "parallel", "arbitrary")),
        name="diff_attn",
    )(q, k, v, k_meta, v_meta, lq1, lk1, lq2, lk2, subln)


def kernel(x, meta_tokens, ffn_norm_pre, ffn_norm_post, ffn_w1, ffn_w3, ffn_w2, mix_norm_pre,
           mix_norm_post, pool_w, pool_b, pool_scale, attn_w_qkv, attn_w_o, attn_lambda_q1,
           attn_lambda_k1, attn_lambda_q2, attn_lambda_k2, attn_subln):
    bsz, seq, d = x.shape
    depth = ffn_w1.shape[0]
    assert d == D_MODEL and meta_tokens.shape == (N_META, d) and depth == 2
    tm = 512
    row = lambda a: a.reshape(1, -1)

    h = x.reshape(bsz * seq, d)
    hm = meta_tokens.astype(x.dtype)

    w1, w3, w2 = ffn_w1.astype(BF16), ffn_w3.astype(BF16), ffn_w2.astype(BF16)

    def ffn_main(h, i, s, **mixer):
        return _ffn(h, row(ffn_norm_pre[i, s]), w1, w3, w2, row(ffn_norm_post[i, s]), layer=i, slot=s,
                    tm=2 * tm, sub=tm, seq=seq, **mixer)

    def ffn_meta(hm, i, s):
        return _ffn(hm, row(ffn_norm_pre[i, s]), w1, w3, w2, row(ffn_norm_post[i, s]), layer=i, slot=s,
                    tm=N_META, sub=N_META)

    h, hm = ffn_main(h, 0, 0), ffn_meta(hm, 0, 0)
    pool_args = (row(mix_norm_pre[0]), pool_w[0].astype(BF16), row(pool_b[0]), row(pool_scale[0]),
                 row(mix_norm_post[0]))
    h = ffn_main(h, 0, 1, mixer="pool", mixer_args=(hm,) + pool_args)
    hm = ffn_meta(_pool_meta(hm, *pool_args), 0, 1)

    h, hm = ffn_main(h, 1, 0), ffn_meta(hm, 1, 0)
    g_pre = row(mix_norm_pre[1])
    w_qkv = attn_w_qkv[0].astype(BF16)
    q, k, v = _qkv(h, g_pre, w_qkv, tm=tm)
    _, k_meta, v_meta = _qkv(hm, g_pre, w_qkv, tm=N_META)
    o = _attention(q, k, v, k_meta, v_meta, row(attn_lambda_q1[0]), row(attn_lambda_k1[0]),
                   row(attn_lambda_q2[0]), row(attn_lambda_k2[0]), row(attn_subln[0]),
                   seq=seq, tq=1024, tk=MXU_TILE, lambda_init=_lambda_init(1))
    h = ffn_main(h, 1, 1, mixer="oproj",
                 mixer_args=(o, attn_w_o[0].astype(BF16), row(mix_norm_post[1])))
    return h.reshape(bsz, seq, d)
```

```python
import functools
import itertools
import math

import jax
import jax.numpy as jnp
from jax import lax
from jax.experimental import pallas as pl
from jax.experimental.pallas import tpu as pltpu

D_MODEL = 1024
N_META = 16
POOL_WINDOWS = (2, 4, 8, 16)
POOL_GROUP = D_MODEL // len(POOL_WINDOWS)
HEAD_DIM = 64
HEAD_WIDTH = 2 * HEAD_DIM
N_HEADS = D_MODEL // HEAD_WIDTH
EPS = 1e-6
HALO = 16
MASK_VALUE = float(jnp.finfo(jnp.float32).min)
LOG2_E = math.log2(math.e)

LANES = 128
MXU_TILE = 256
VMEM_LIMIT_BYTES = 56 * 1024 * 1024

F32 = jnp.float32
BF16 = jnp.bfloat16


def _lambda_init(layer_idx):
    return 0.8 - 0.6 * math.exp(-0.3 * layer_idx)


def _rmsnorm(x, g):
    return x * lax.rsqrt(jnp.mean(x * x, axis=-1, keepdims=True) + EPS) * g


def _resident(shape):
    return pl.BlockSpec(shape, lambda *_: (0,) * len(shape), pipeline_mode=pl.Buffered(1))


def _params(semantics):
    return pltpu.CompilerParams(dimension_semantics=semantics, vmem_limit_bytes=VMEM_LIMIT_BYTES)


N_MIXER_REFS = {None: 0, "pool": 7, "oproj": 3}


def _ffn_kernel(*refs, f_chunks, sub, mixer):
    mixer_refs = refs[:N_MIXER_REFS[mixer]]
    h_ref, gpre_ref, w1_ref, w3_ref, w2_ref, gpost_ref, o_ref, acc_ref = refs[N_MIXER_REFS[mixer]:]
    groups = [slice(r0, r0 + sub) for r0 in range(0, h_ref.shape[0], sub)]

    def mixer_steps(rows):
        h = h_ref[rows, :]
        if mixer == "pool":
            prev_ref, meta_ref, mgpre_ref, pw_ref, pb_ref, pscale_ref, mgpost_ref = mixer_refs
            if rows.start == 0:
                halo = jnp.where(pl.program_id(1) == 0, meta_ref[...], prev_ref[...])
            else:
                halo = h_ref[rows.start - HALO:rows.start, :]
            for out in _pool_mix_steps(h, halo, mgpre_ref[...], pw_ref, pb_ref, pscale_ref,
                                       mgpost_ref[...]):
                if out is not None:
                    o_ref[rows, :] = out
                yield
        elif mixer == "oproj":
            attn_ref, wo_ref, mgpost_ref = mixer_refs
            m = jnp.dot(attn_ref[rows, :], wo_ref[...], preferred_element_type=F32)
            o_ref[rows, :] = h + _rmsnorm(m, mgpost_ref[...])
            yield

    def ffn_steps(rows):
        h = h_ref[rows, :] if mixer is None else o_ref[rows, :]
        r = lax.rsqrt(jnp.mean(h * h, axis=-1, keepdims=True) + EPS)
        hg = (h * gpre_ref[...]).astype(BF16)
        for idx, (lo, size) in enumerate(f_chunks):
            a = jnp.dot(hg, w1_ref[:, lo:lo + size], preferred_element_type=F32) * r
            b = jnp.dot(hg, w3_ref[:, lo:lo + size], preferred_element_type=F32) * r
            g = (jax.nn.silu(a) * b).astype(BF16)
            part = jnp.dot(g, w2_ref[lo:lo + size, :], preferred_element_type=F32)
            if idx == 0:
                acc_ref[rows, :] = part
            else:
                acc_ref[rows, :] += part
            yield
        o_ref[rows, :] = h + 0.5 * _rmsnorm(acc_ref[rows, :], gpost_ref[...])
        yield

    for _ in mixer_steps(groups[0]):
        pass
    for rows, nxt in zip(groups, groups[1:] + [None]):
        for _ in itertools.zip_longest(ffn_steps(rows), mixer_steps(nxt) if nxt else ()):
            pass


def _ffn(h, gpre, w1, w3, w2, gpost, *, layer, slot, tm, sub, seq=None, mixer=None, mixer_args=()):
    n, d = h.shape
    f = w1.shape[-1]
    seq = n if seq is None else seq
    tiles = seq // tm
    f_chunk = 512 if mixer == "pool" else 1024
    f_chunks = tuple((lo, min(f_chunk, f - lo)) for lo in range(0, f, f_chunk))
    row = pl.BlockSpec((tm, d), lambda b, i: (b * tiles + i, 0))
    vec = _resident((1, d))

    def weight(rows, cols):
        return pl.BlockSpec((None, None, rows, cols), lambda b, i: (layer, slot, 0, 0),
                            pipeline_mode=pl.Buffered(1))

    if mixer == "pool":
        prev = pl.BlockSpec(
            (HALO, d), lambda b, i: (jnp.maximum(b * (seq // HALO) + i * (tm // HALO) - 1, 0), 0))
        mixer_ops = (h,) + tuple(mixer_args)
        mixer_specs = [prev, _resident((N_META, d)), vec, _resident(mixer_args[2].shape), vec, vec, vec]
    elif mixer == "oproj":
        mixer_ops = tuple(mixer_args)
        mixer_specs = [row, _resident((d, d)), vec]
    else:
        mixer_ops, mixer_specs = (), []
    assert len(mixer_specs) == N_MIXER_REFS[mixer]

    return pl.pallas_call(
        functools.partial(_ffn_kernel, f_chunks=f_chunks, sub=sub, mixer=mixer),
        out_shape=jax.ShapeDtypeStruct((n, d), F32),
        grid=(n // seq, tiles),
        in_specs=mixer_specs + [row, vec, weight(d, f), weight(d, f), weight(f, d), vec],
        out_specs=row,
        scratch_shapes=[pltpu.VMEM((tm, d), F32)],
        compiler_params=_params(("parallel", "parallel")),
        name="ffn" if mixer is None else "ffn_" + mixer,
    )(*mixer_ops, h, gpre, w1, w3, w2, gpost)


def _pool_mix_steps(h, halo, gpre, w_ref, b_ref, scale_ref, gpost):
    rows = h.shape[0]
    is_meta = halo is None
    hn = _rmsnorm(h, gpre)
    halo_n = jnp.zeros((HALO, D_MODEL), F32) if is_meta else _rmsnorm(halo, gpre)
    ext = jnp.concatenate([halo_n, hn], axis=0)
    yield None
    ys = []
    for gi, win in enumerate(POOL_WINDOWS):
        cols = slice(gi * POOL_GROUP, (gi + 1) * POOL_GROUP)
        acc = ext[:, cols]
        step = 1
        while step < win:
            acc = acc + pltpu.roll(acc, step, axis=0)
            step *= 2
        wsum = acc[HALO:]
        if is_meta:
            pos = lax.broadcasted_iota(jnp.int32, (rows, POOL_GROUP), 0)
            pooled = wsum / jnp.minimum(pos + 1, win).astype(F32)
        else:
            pooled = wsum * (1.0 / win)
        diff = (pooled - hn[:, cols]).astype(BF16)
        y = jnp.dot(diff, w_ref[gi], preferred_element_type=F32) + b_ref[:, cols]
        ys.append(y * scale_ref[:, cols])
        yield None
    m = jnp.concatenate(ys, axis=1)
    yield h + _rmsnorm(m, gpost)


def _pool_meta_kernel(h_ref, gpre_ref, w_ref, b_ref, scale_ref, gpost_ref, o_ref):
    *_, o_ref[...] = _pool_mix_steps(h_ref[...], None, gpre_ref[...], w_ref, b_ref, scale_ref, gpost_ref[...])


def _pool_meta(h, gpre, w, b, scale, gpost):
    n, d = h.shape
    vec = _resident((1, d))
    return pl.pallas_call(
        _pool_meta_kernel,
        out_shape=jax.ShapeDtypeStruct((n, d), F32),
        grid=(1,),
        in_specs=[_resident((n, d)), vec, _resident(w.shape), vec, vec, vec],
        out_specs=pl.BlockSpec((n, d), lambda i: (0, 0)),
        compiler_params=_params(("arbitrary",)),
        name="pool_meta",
    )(h, gpre, w, b, scale, gpost)


def _qkv_kernel(h_ref, g_ref, w_ref, q_ref, k_ref, v_ref):
    hn = _rmsnorm(h_ref[...], g_ref[...]).astype(BF16)
    d = D_MODEL
    q = jnp.dot(hn, w_ref[:, 0:d], preferred_element_type=F32)
    q_ref[...] = (q * (HEAD_DIM ** -0.5 * LOG2_E)).astype(BF16)
    k_ref[...] = jnp.dot(hn, w_ref[:, d:2 * d], preferred_element_type=F32).astype(BF16)
    v_ref[...] = jnp.dot(hn, w_ref[:, 2 * d:3 * d], preferred_element_type=F32).astype(BF16)


def _qkv(h, g, w, *, tm):
    n, d = h.shape
    row = pl.BlockSpec((tm, d), lambda i: (i, 0))
    out = jax.ShapeDtypeStruct((n, d), BF16)
    return pl.pallas_call(
        _qkv_kernel,
        out_shape=(out, out, out),
        grid=(n // tm,),
        in_specs=[row, _resident((1, d)), _resident((d, 3 * d))],
        out_specs=(row, row, row),
        compiler_params=_params(("parallel",)),
        name="qkv",
    )(h, g, w)


def _attn_kernel(q_ref, k_ref, v_ref, km_ref, vm_ref, lq1_ref, lk1_ref, lq2_ref, lk2_ref, subln_ref,
                 o_ref, qq_ref, vx_ref, kmx_ref, vmx_ref, m_ref, acc_ref, *, tq, tk, lambda_init):
    qi = pl.program_id(2)
    nt = (((1,), (1,)), ((), ()))
    hw = HEAD_WIDTH

    @pl.when(qi == 0)
    def _stage_keys_values():
        vx_ref[:, 0:hw] = v_ref[...]
        vx_ref[:, hw:2 * hw] = jnp.ones((v_ref.shape[0], hw), BF16)
        kmx_ref[...] = jnp.zeros(kmx_ref.shape, BF16)
        kmx_ref[0:N_META, :] = km_ref[...]
        vmx_ref[...] = jnp.zeros(vmx_ref.shape, BF16)
        vmx_ref[0:N_META, 0:hw] = vm_ref[...]
        vmx_ref[0:N_META, hw:2 * hw] = jnp.ones((N_META, hw), BF16)

    q = q_ref[...]
    lane = lax.broadcasted_iota(jnp.int32, q.shape, 1)
    zero = jnp.zeros_like(q)
    qq_ref[0:tq, :] = jnp.where(lane < HEAD_DIM, q, zero)
    qq_ref[tq:2 * tq, :] = jnp.where(lane >= HEAD_DIM, q, zero)

    def scores(item):
        r0, nrows, keys_ref, _, k0, n, masks, _ = item
        s = lax.dot_general(qq_ref[r0:r0 + nrows, :], keys_ref[k0:k0 + n, :], nt,
                            preferred_element_type=F32)
        blocks = []
        for j, keep in enumerate(masks):
            blk = s[:, j * LANES:(j + 1) * LANES]
            blocks.append(blk if keep is None else jnp.where(keep, blk, MASK_VALUE))
        m_tile = jnp.max(functools.reduce(jnp.maximum, blocks), axis=1, keepdims=True)
        return blocks, m_tile

    def probabilities(item, blocks, m_tile):
        r0, nrows, _, _, _, _, _, first = item
        rows = slice(r0, r0 + nrows)
        if first:
            m_new, alpha = jnp.broadcast_to(m_tile, (nrows, LANES)), None
        else:
            m_prev = m_ref[rows, :]
            m_new = jnp.maximum(m_prev, m_tile)
            alpha = jnp.exp2(m_prev - m_new)
        m_ref[rows, :] = m_new
        p = jnp.concatenate([jnp.exp2(blk - m_new) for blk in blocks], axis=1).astype(BF16)
        return p, alpha

    def accumulate(item, p, alpha):
        r0, nrows, _, values_ref, k0, n, _, first = item
        rows = slice(r0, r0 + nrows)
        pv = jnp.dot(p, values_ref[k0:k0 + n, :], preferred_element_type=F32)
        if first:
            acc_ref[rows, :] = pv
        else:
            acc_ref[rows, :] = jnp.concatenate([alpha, alpha], axis=1) * acc_ref[rows, :] + pv

    def query_tile(i):
        start = i * tq
        meta_cols = lax.broadcasted_iota(jnp.int32, (tq, LANES), 1)
        items = [(half * tq, tq, kmx_ref, vmx_ref, 0, LANES, [meta_cols < N_META], True)
                 for half in range(2)]
        for lo in range(0, start, 2 * tk):
            for r0 in range(0, 2 * tq, tk):
                items.append((r0, tk, k_ref, vx_ref, lo, 2 * tk, [None] * (2 * tk // LANES), False))
        for lo in range(0, tq, tk):
            nrows = tq - lo
            row_id = lax.broadcasted_iota(jnp.int32, (nrows, LANES), 0)
            col_id = lax.broadcasted_iota(jnp.int32, (nrows, LANES), 1)
            tri = [col_id + j * LANES <= row_id for j in range(tk // LANES)]
            for half in range(2):
                items.append((half * tq + lo, nrows, k_ref, vx_ref, start + lo, tk, tri, False))
        scored, weighted = {}, {}
        for t in range(len(items) + 2):
            if t < len(items):
                scored[t] = scores(items[t])
            if 0 <= t - 1 < len(items):
                weighted[t - 1] = probabilities(items[t - 1], *scored.pop(t - 1))
            if 0 <= t - 2 < len(items):
                accumulate(items[t - 2], *weighted.pop(t - 2))

    for i in range(k_ref.shape[0] // tq):
        pl.when(qi == i)(functools.partial(query_tile, i))

    lam = (jnp.exp(jnp.sum(lq1_ref[...] * lk1_ref[...], axis=1, keepdims=True))
           - jnp.exp(jnp.sum(lq2_ref[...] * lk2_ref[...], axis=1, keepdims=True)) + lambda_init)
    acc = acc_ref[...]
    o = acc[:, 0:hw] / acc[:, hw:2 * hw]
    o = o[0:tq] - lam * o[tq:2 * tq]
    o_ref[...] = (_rmsnorm(o, subln_ref[...]) * (1.0 - lambda_init)).astype(o_ref.dtype)


def _attention(q, k, v, k_meta, v_meta, lq1, lk1, lq2, lk2, subln, *, seq, tq, tk, lambda_init):
    n, d = q.shape
    nq = seq // tq
    hw = HEAD_WIDTH
    lam_spec = _resident((1, HEAD_DIM))
    return pl.pallas_call(
        functools.partial(_attn_kernel, tq=tq, tk=tk, lambda_init=lambda_init),
        out_shape=jax.ShapeDtypeStruct((n, d), BF16),
        grid=(n // seq, N_HEADS, nq),
        in_specs=[
            pl.BlockSpec((tq, hw), lambda b, h, i: (b * nq + i, h)),
            pl.BlockSpec((seq, hw), lambda b, h, i: (b, h)),
            pl.BlockSpec((seq, hw), lambda b, h, i: (b, h)),
            pl.BlockSpec((N_META, hw), lambda b, h, i: (0, h)),
            pl.BlockSpec((N_META, hw), lambda b, h, i: (0, h)),
            lam_spec, lam_spec, lam_spec, lam_spec, _resident((1, hw)),
        ],
        out_specs=pl.BlockSpec((tq, hw), lambda b, h, i: (b * nq + i, h)),
        scratch_shapes=[
            pltpu.VMEM((2 * tq, hw), BF16),
            pltpu.VMEM((seq, 2 * hw), BF16),
            pltpu.VMEM((LANES, hw), BF16),
            pltpu.VMEM((LANES, 2 * hw), BF16),
            pltpu.VMEM((2 * tq, LANES), F32),
            pltpu.VMEM((2 * tq, 2 * hw), F32),
        ],
        compiler_params=_params(("parallel", "parallel", "arbitrary")),
        name="diff_attn",
    )(q, k, v, k_meta, v_meta, lq1, lk1, lq2, lk2, subln)


def kernel(x, meta_tokens, ffn_norm_pre, ffn_norm_post, ffn_w1, ffn_w3, ffn_w2, mix_norm_pre,
           mix_norm_post, pool_w, pool_b, pool_scale, attn_w_qkv, attn_w_o, attn_lambda_q1,
           attn_lambda_k1, attn_lambda_q2, attn_lambda_k2, attn_subln):
    bsz, seq, d = x.shape
    depth = ffn_w1.shape[0]
    assert d == D_MODEL and meta_tokens.shape == (N_META, d) and depth == 2
    tm = 512
    row = lambda a: a.reshape(1, -1)

    h = x.reshape(bsz * seq, d)
    hm = meta_tokens.astype(x.dtype)

    w1, w3, w2 = ffn_w1.astype(BF16), ffn_w3.astype(BF16), ffn_w2.astype(BF16)

    def ffn_main(h, i, s, **mixer):
        return _ffn(h, row(ffn_norm_pre[i, s]), w1, w3, w2, row(ffn_norm_post[i, s]), layer=i, slot=s,
                    tm=2 * tm, sub=tm, seq=seq, **mixer)

    def ffn_meta(hm, i, s):
        return _ffn(hm, row(ffn_norm_pre[i, s]), w1, w3, w2, row(ffn_norm_post[i, s]), layer=i, slot=s,
                    tm=N_META, sub=N_META)

    h, hm = ffn_main(h, 0, 0), ffn_meta(hm, 0, 0)
    pool_args = (row(mix_norm_pre[0]), pool_w[0].astype(BF16), row(pool_b[0]), row(pool_scale[0]),
                 row(mix_norm_post[0]))
    h = ffn_main(h, 0, 1, mixer="pool", mixer_args=(hm,) + pool_args)
    hm = ffn_meta(_pool_meta(hm, *pool_args), 0, 1)

    h, hm = ffn_main(h, 1, 0), ffn_meta(hm, 1, 0)
    g_pre = row(mix_norm_pre[1])
    w_qkv = attn_w_qkv[0].astype(BF16)
    q, k, v = _qkv(h, g_pre, w_qkv, tm=tm)
    _, k_meta, v_meta = _qkv(hm, g_pre, w_qkv, tm=N_META)
    o = _attention(q, k, v, k_meta, v_meta, row(attn_lambda_q1[0]), row(attn_lambda_k1[0]),
                   row(attn_lambda_q2[0]), row(attn_lambda_k2[0]), row(attn_subln[0]),
                   seq=seq, tq=1024, tk=MXU_TILE, lambda_init=_lambda_init(1))
    h = ffn_main(h, 1, 1, mixer="oproj",
                 mixer_args=(o, attn_w_o[0].astype(BF16), row(mix_norm_post[1])))
    return h.reshape(bsz, seq, d)
```

```python
import functools
import itertools
import math

import jax
import jax.numpy as jnp
from jax import lax
from jax.experimental import pallas as pl
from jax.experimental.pallas import tpu as pltpu

D_MODEL = 1024
N_META = 16
POOL_WINDOWS = (2, 4, 8, 16)
POOL_GROUP = D_MODEL // len(POOL_WINDOWS)
HEAD_DIM = 64
HEAD_WIDTH = 2 * HEAD_DIM
N_HEADS = D_MODEL // HEAD_WIDTH
EPS = 1e-6
HALO = 16
MASK_VALUE = float(jnp.finfo(jnp.float32).min)
LOG2_E = math.log2(math.e)

LANES = 128
MXU_TILE = 256
ROW_TILE = 1024
ROW_GROUP = 512
VMEM_LIMIT_BYTES = 56 * 1024 * 1024

F32 = jnp.float32
BF16 = jnp.bfloat16


def _lambda_init(layer_idx):
    return 0.8 - 0.6 * math.exp(-0.3 * layer_idx)


def _rmsnorm(x, g):
    return x * lax.rsqrt(jnp.mean(x * x, axis=-1, keepdims=True) + EPS) * g


def _resident(shape):
    return pl.BlockSpec(shape, lambda *_: (0,) * len(shape), pipeline_mode=pl.Buffered(1))


def _params(semantics):
    return pltpu.CompilerParams(dimension_semantics=semantics, vmem_limit_bytes=VMEM_LIMIT_BYTES)


N_MIXER_REFS = {None: 0, "pool": 7, "oproj": 3}


def _ffn_kernel(*refs, f_chunks, sub, mixer):
    mixer_refs = refs[:N_MIXER_REFS[mixer]]
    h_ref, gpre_ref, w1_ref, w3_ref, w2_ref, gpost_ref, o_ref, acc_ref = refs[N_MIXER_REFS[mixer]:]
    groups = [slice(r0, r0 + sub) for r0 in range(0, h_ref.shape[0], sub)]

    def mixer_steps(rows):
        h = h_ref[rows, :]
        if mixer == "pool":
            prev_ref, meta_ref, mgpre_ref, pw_ref, pb_ref, pscale_ref, mgpost_ref = mixer_refs
            if rows.start == 0:
                halo = jnp.where(pl.program_id(1) == 0, meta_ref[...], prev_ref[...])
            else:
                halo = h_ref[rows.start - HALO:rows.start, :]
            for out in _pool_mix_steps(h, halo, mgpre_ref[...], pw_ref, pb_ref, pscale_ref,
                                       mgpost_ref[...]):
                if out is not None:
                    o_ref[rows, :] = out
                yield
        elif mixer == "oproj":
            attn_ref, wo_ref, mgpost_ref = mixer_refs
            m = jnp.dot(attn_ref[rows, :], wo_ref[...], preferred_element_type=F32)
            o_ref[rows, :] = h + _rmsnorm(m, mgpost_ref[...])
            yield

    def ffn_steps(rows):
        h = h_ref[rows, :] if mixer is None else o_ref[rows, :]
        r = lax.rsqrt(jnp.mean(h * h, axis=-1, keepdims=True) + EPS)
        hg = (h * gpre_ref[...]).astype(BF16)
        for idx, (lo, size) in enumerate(f_chunks):
            a = jnp.dot(hg, w1_ref[:, lo:lo + size], preferred_element_type=F32) * r
            b = jnp.dot(hg, w3_ref[:, lo:lo + size], preferred_element_type=F32) * r
            g = (jax.nn.silu(a) * b).astype(BF16)
            part = jnp.dot(g, w2_ref[lo:lo + size, :], preferred_element_type=F32)
            if idx == 0:
                acc_ref[rows, :] = part
            else:
                acc_ref[rows, :] += part
            yield
        o_ref[rows, :] = h + 0.5 * _rmsnorm(acc_ref[rows, :], gpost_ref[...])
        yield

    for _ in mixer_steps(groups[0]):
        pass
    for rows, nxt in zip(groups, groups[1:] + [None]):
        for _ in itertools.zip_longest(ffn_steps(rows), mixer_steps(nxt) if nxt else ()):
            pass


def _ffn(h, gpre, w1, w3, w2, gpost, *, layer, slot, tm, sub, seq=None, mixer=None, mixer_args=()):
    n, d = h.shape
    f = w1.shape[-1]
    seq = n if seq is None else seq
    tiles = seq // tm
    f_chunk = 512 if mixer == "pool" else 1024
    f_chunks = tuple((lo, min(f_chunk, f - lo)) for lo in range(0, f, f_chunk))
    row = pl.BlockSpec((tm, d), lambda b, i: (b * tiles + i, 0))
    vec = _resident((1, d))

    def weight(rows, cols):
        return pl.BlockSpec((None, None, rows, cols), lambda b, i: (layer, slot, 0, 0),
                            pipeline_mode=pl.Buffered(1))

    if mixer == "pool":
        prev = pl.BlockSpec(
            (HALO, d), lambda b, i: (jnp.maximum(b * (seq // HALO) + i * (tm // HALO) - 1, 0), 0))
        mixer_ops = (h,) + tuple(mixer_args)
        mixer_specs = [prev, _resident((N_META, d)), vec, _resident(mixer_args[2].shape), vec, vec, vec]
    elif mixer == "oproj":
        mixer_ops = tuple(mixer_args)
        mixer_specs = [row, _resident((d, d)), vec]
    else:
        mixer_ops, mixer_specs = (), []
    assert len(mixer_specs) == N_MIXER_REFS[mixer]

    return pl.pallas_call(
        functools.partial(_ffn_kernel, f_chunks=f_chunks, sub=sub, mixer=mixer),
        out_shape=jax.ShapeDtypeStruct((n, d), F32),
        grid=(n // seq, tiles),
        in_specs=mixer_specs + [row, vec, weight(d, f), weight(d, f), weight(f, d), vec],
        out_specs=row,
        scratch_shapes=[pltpu.VMEM((tm, d), F32)],
        compiler_params=_params(("parallel", "parallel")),
        name="ffn" if mixer is None else "ffn_" + mixer,
    )(*mixer_ops, h, gpre, w1, w3, w2, gpost)


def _pool_mix_steps(h, halo, gpre, w_ref, b_ref, scale_ref, gpost):
    rows = h.shape[0]
    is_meta = halo is None
    hn = _rmsnorm(h, gpre)
    halo_n = jnp.zeros((HALO, D_MODEL), F32) if is_meta else _rmsnorm(halo, gpre)
    ext = jnp.concatenate([halo_n, hn], axis=0)
    yield None
    ys = []
    for gi, win in enumerate(POOL_WINDOWS):
        cols = slice(gi * POOL_GROUP, (gi + 1) * POOL_GROUP)
        acc = ext[:, cols]
        step = 1
        while step < win:
            acc = acc + pltpu.roll(acc, step, axis=0)
            step *= 2
        wsum = acc[HALO:]
        if is_meta:
            pos = lax.broadcasted_iota(jnp.int32, (rows, POOL_GROUP), 0)
            pooled = wsum / jnp.minimum(pos + 1, win).astype(F32)
        else:
            pooled = wsum * (1.0 / win)
        diff = (pooled - hn[:, cols]).astype(BF16)
        y = jnp.dot(diff, w_ref[gi], preferred_element_type=F32) + b_ref[:, cols]
        ys.append(y * scale_ref[:, cols])
        yield None
    m = jnp.concatenate(ys, axis=1)
    yield h + _rmsnorm(m, gpost)


def _pool_meta_kernel(h_ref, gpre_ref, w_ref, b_ref, scale_ref, gpost_ref, o_ref):
    *_, o_ref[...] = _pool_mix_steps(h_ref[...], None, gpre_ref[...], w_ref, b_ref, scale_ref, gpost_ref[...])


def _pool_meta(h, gpre, w, b, scale, gpost):
    n, d = h.shape
    vec = _resident((1, d))
    return pl.pallas_call(
        _pool_meta_kernel,
        out_shape=jax.ShapeDtypeStruct((n, d), F32),
        grid=(1,),
        in_specs=[_resident((n, d)), vec, _resident(w.shape), vec, vec, vec],
        out_specs=pl.BlockSpec((n, d), lambda i: (0, 0)),
        compiler_params=_params(("arbitrary",)),
        name="pool_meta",
    )(h, gpre, w, b, scale, gpost)


def _qkv_kernel(h_ref, g_ref, w_ref, q_ref, k_ref, v_ref, *, sub):
    d = D_MODEL
    for r0 in range(0, h_ref.shape[0], sub):
        rows = slice(r0, r0 + sub)
        h = h_ref[rows, :]
        r = lax.rsqrt(jnp.mean(h * h, axis=-1, keepdims=True) + EPS)
        hg = (h * g_ref[...]).astype(BF16)
        q = jnp.dot(hg, w_ref[:, 0:d], preferred_element_type=F32)
        q_ref[rows, :] = (q * (r * (HEAD_DIM ** -0.5 * LOG2_E))).astype(BF16)
        k = jnp.dot(hg, w_ref[:, d:2 * d], preferred_element_type=F32)
        k_ref[rows, :] = (k * r).astype(BF16)
        v = jnp.dot(hg, w_ref[:, 2 * d:3 * d], preferred_element_type=F32)
        v_ref[rows, :] = (v * r).astype(BF16)


def _qkv(h, g, w, *, tm, sub):
    n, d = h.shape
    row = pl.BlockSpec((tm, d), lambda i: (i, 0))
    out = jax.ShapeDtypeStruct((n, d), BF16)
    return pl.pallas_call(
        functools.partial(_qkv_kernel, sub=sub),
        out_shape=(out, out, out),
        grid=(n // tm,),
        in_specs=[row, _resident((1, d)), _resident((d, 3 * d))],
        out_specs=(row, row, row),
        compiler_params=_params(("parallel",)),
        name="qkv",
    )(h, g, w)


def _attn_kernel(q_ref, k_ref, v_ref, km_ref, vm_ref, lq1_ref, lk1_ref, lq2_ref, lk2_ref, subln_ref,
                 o_ref, qq_ref, vx_ref, kmx_ref, vmx_ref, m_ref, acc_ref, *, tq, tk, lambda_init):
    qi = pl.program_id(2)
    nt = (((1,), (1,)), ((), ()))
    hw = HEAD_WIDTH

    @pl.when(qi == 0)
    def _stage_keys_values():
        vx_ref[:, 0:hw] = v_ref[...]
        vx_ref[:, hw:2 * hw] = jnp.ones((v_ref.shape[0], hw), BF16)
        kmx_ref[...] = jnp.zeros(kmx_ref.shape, BF16)
        kmx_ref[0:N_META, :] = km_ref[...]
        vmx_ref[...] = jnp.zeros(vmx_ref.shape, BF16)
        vmx_ref[0:N_META, 0:hw] = vm_ref[...]
        vmx_ref[0:N_META, hw:2 * hw] = jnp.ones((N_META, hw), BF16)

    q = q_ref[...]
    lane = lax.broadcasted_iota(jnp.int32, q.shape, 1)
    zero = jnp.zeros_like(q)
    qq_ref[0:tq, :] = jnp.where(lane < HEAD_DIM, q, zero)
    qq_ref[tq:2 * tq, :] = jnp.where(lane >= HEAD_DIM, q, zero)

    def scores(item):
        r0, nrows, keys_ref, _, k0, n, masks, _ = item
        s = lax.dot_general(qq_ref[r0:r0 + nrows, :], keys_ref[k0:k0 + n, :], nt,
                            preferred_element_type=F32)
        blocks = []
        for j, keep in enumerate(masks):
            blk = s[:, j * LANES:(j + 1) * LANES]
            blocks.append(blk if keep is None else jnp.where(keep, blk, MASK_VALUE))
        m_tile = jnp.max(functools.reduce(jnp.maximum, blocks), axis=1, keepdims=True)
        return blocks, m_tile

    def accumulate(item, blocks, m_tile):
        r0, nrows, _, values_ref, k0, n, _, first = item
        vx = values_ref[k0:k0 + n, :]
        rows = slice(r0, r0 + nrows)
        if first:
            m_new = jnp.broadcast_to(m_tile, (nrows, LANES))
        else:
            m_prev = m_ref[rows, :]
            m_new = jnp.maximum(m_prev, m_tile)
            alpha = jnp.exp2(m_prev - m_new)
        p = jnp.concatenate([jnp.exp2(blk - m_new) for blk in blocks], axis=1).astype(BF16)
        pv = jnp.dot(p, vx, preferred_element_type=F32)
        if first:
            acc_ref[rows, :] = pv
        else:
            acc_ref[rows, :] = jnp.concatenate([alpha, alpha], axis=1) * acc_ref[rows, :] + pv
        m_ref[rows, :] = m_new

    def query_tile(i):
        start = i * tq
        meta_cols = lax.broadcasted_iota(jnp.int32, (tq, LANES), 1)
        items = [(half * tq, tq, kmx_ref, vmx_ref, 0, LANES, [meta_cols < N_META], True)
                 for half in range(2)]
        for lo in range(0, start, 2 * tk):
            for r0 in range(0, 2 * tq, tk):
                items.append((r0, tk, k_ref, vx_ref, lo, 2 * tk, [None] * (2 * tk // LANES), False))
        for lo in range(0, tq, tk):
            nrows = tq - lo
            row_id = lax.broadcasted_iota(jnp.int32, (nrows, LANES), 0)
            col_id = lax.broadcasted_iota(jnp.int32, (nrows, LANES), 1)
            tri = [col_id + j * LANES <= row_id for j in range(tk // LANES)]
            for half in range(2):
                items.append((half * tq + lo, nrows, k_ref, vx_ref, start + lo, tk, tri, False))
        pending = None
        for item in items:
            staged = scores(item)
            if pending is not None:
                accumulate(*pending)
            pending = (item,) + staged
        accumulate(*pending)

    for i in range(k_ref.shape[0] // tq):
        pl.when(qi == i)(functools.partial(query_tile, i))

    lam = (jnp.exp(jnp.sum(lq1_ref[...] * lk1_ref[...], axis=1, keepdims=True))
           - jnp.exp(jnp.sum(lq2_ref[...] * lk2_ref[...], axis=1, keepdims=True)) + lambda_init)
    acc = acc_ref[...]
    o = acc[:, 0:hw] / acc[:, hw:2 * hw]
    o = o[0:tq] - lam * o[tq:2 * tq]
    o_ref[...] = (_rmsnorm(o, subln_ref[...]) * (1.0 - lambda_init)).astype(o_ref.dtype)


def _attention(q, k, v, k_meta, v_meta, lq1, lk1, lq2, lk2, subln, *, seq, tq, tk, lambda_init):
    n, d = q.shape
    nq = seq // tq
    hw = HEAD_WIDTH
    lam_spec = _resident((1, HEAD_DIM))
    return pl.pallas_call(
        functools.partial(_attn_kernel, tq=tq, tk=tk, lambda_init=lambda_init),
        out_shape=jax.ShapeDtypeStruct((n, d), BF16),
        grid=(n // seq, N_HEADS, nq),
        in_specs=[
            pl.BlockSpec((tq, hw), lambda b, h, i: (b * nq + i, h)),
            pl.BlockSpec((seq, hw), lambda b, h, i: (b, h)),
            pl.BlockSpec((seq, hw), lambda b, h, i: (b, h)),
            pl.BlockSpec((N_META, hw), lambda b, h, i: (0, h)),
            pl.BlockSpec((N_META, hw), lambda b, h, i: (0, h)),
            lam_spec, lam_spec, lam_spec, lam_spec, _resident((1, hw)),
        ],
        out_specs=pl.BlockSpec((tq, hw), lambda b, h, i: (b * nq + i, h)),
        scratch_shapes=[
            pltpu.VMEM((2 * tq, hw), BF16),
            pltpu.VMEM((seq, 2 * hw), BF16),
            pltpu.VMEM((LANES, hw), BF16),
            pltpu.VMEM((LANES, 2 * hw), BF16),
            pltpu.VMEM((2 * tq, LANES), F32),
            pltpu.VMEM((2 * tq, 2 * hw), F32),
        ],
        compiler_params=_params(("parallel", "parallel", "arbitrary")),
        name="diff_attn",
    )(q, k, v, k_meta, v_meta, lq1, lk1, lq2, lk2, subln)


def kernel(x, meta_tokens, ffn_norm_pre, ffn_norm_post, ffn_w1, ffn_w3, ffn_w2, mix_norm_pre,
           mix_norm_post, pool_w, pool_b, pool_scale, attn_w_qkv, attn_w_o, attn_lambda_q1,
           attn_lambda_k1, attn_lambda_q2, attn_lambda_k2, attn_subln):
    bsz, seq, d = x.shape
    depth = ffn_w1.shape[0]
    assert d == D_MODEL and meta_tokens.shape == (N_META, d) and depth == 2
    tile, group = ROW_TILE, ROW_GROUP
    assert seq % tile == 0 and tile % group == 0
    row = lambda a: a.reshape(1, -1)

    h = x.reshape(bsz * seq, d)
    hm = meta_tokens.astype(x.dtype)

    w1, w3, w2 = ffn_w1.astype(BF16), ffn_w3.astype(BF16), ffn_w2.astype(BF16)

    def ffn_main(h, i, s, **mixer):
        return _ffn(h, row(ffn_norm_pre[i, s]), w1, w3, w2, row(ffn_norm_post[i, s]), layer=i, slot=s,
                    tm=tile, sub=group, seq=seq, **mixer)

    def ffn_meta(hm, i, s):
        return _ffn(hm, row(ffn_norm_pre[i, s]), w1, w3, w2, row(ffn_norm_post[i, s]), layer=i, slot=s,
                    tm=N_META, sub=N_META)

    h, hm = ffn_main(h, 0, 0), ffn_meta(hm, 0, 0)
    pool_args = (row(mix_norm_pre[0]), pool_w[0].astype(BF16), row(pool_b[0]), row(pool_scale[0]),
                 row(mix_norm_post[0]))
    h = ffn_main(h, 0, 1, mixer="pool", mixer_args=(hm,) + pool_args)
    hm = ffn_meta(_pool_meta(hm, *pool_args), 0, 1)

    h, hm = ffn_main(h, 1, 0), ffn_meta(hm, 1, 0)
    g_pre = row(mix_norm_pre[1])
    w_qkv = attn_w_qkv[0].astype(BF16)
    q, k, v = _qkv(h, g_pre, w_qkv, tm=tile, sub=group)
    _, k_meta, v_meta = _qkv(hm, g_pre, w_qkv, tm=N_META, sub=N_META)
    o = _attention(q, k, v, k_meta, v_meta, row(attn_lambda_q1[0]), row(attn_lambda_k1[0]),
                   row(attn_lambda_q2[0]), row(attn_lambda_k2[0]), row(attn_subln[0]),
                   seq=seq, tq=tile, tk=MXU_TILE, lambda_init=_lambda_init(1))
    h = ffn_main(h, 1, 1, mixer="oproj",
                 mixer_args=(o, attn_w_o[0].astype(BF16), row(mix_norm_post[1])))
    return h.reshape(bsz, seq, d)
```

```python
import functools
import itertools
import math

import jax
import jax.numpy as jnp
from jax import lax
from jax.experimental import pallas as pl
from jax.experimental.pallas import tpu as pltpu

D_MODEL = 1024
N_META = 16
POOL_WINDOWS = (2, 4, 8, 16)
POOL_GROUP = D_MODEL // len(POOL_WINDOWS)
HEAD_DIM = 64
HEAD_WIDTH = 2 * HEAD_DIM
N_HEADS = D_MODEL // HEAD_WIDTH
EPS = 1e-6
HALO = 16
MASK_VALUE = float(jnp.finfo(jnp.float32).min)
LOG2_E = math.log2(math.e)

LANES = 128
MXU_TILE = 256
ROW_TILE = 1024
ROW_GROUP = 512
VMEM_LIMIT_BYTES = 56 * 1024 * 1024

F32 = jnp.float32
BF16 = jnp.bfloat16


def _lambda_init(layer_idx):
    return 0.8 - 0.6 * math.exp(-0.3 * layer_idx)


def _rmsnorm(x, g):
    return x * lax.rsqrt(jnp.mean(x * x, axis=-1, keepdims=True) + EPS) * g


def _resident(shape):
    return pl.BlockSpec(shape, lambda *_: (0,) * len(shape), pipeline_mode=pl.Buffered(1))


def _params(semantics):
    return pltpu.CompilerParams(dimension_semantics=semantics, vmem_limit_bytes=VMEM_LIMIT_BYTES)


N_MIXER_REFS = {None: 0, "pool": 7, "oproj": 3}


def _ffn_kernel(*refs, f_chunks, sub, mixer):
    mixer_refs = refs[:N_MIXER_REFS[mixer]]
    h_ref, gpre_ref, w1_ref, w3_ref, w2_ref, gpost_ref, o_ref, acc_ref = refs[N_MIXER_REFS[mixer]:]
    groups = [slice(r0, r0 + sub) for r0 in range(0, h_ref.shape[0], sub)]

    def mixer_steps(rows):
        h = h_ref[rows, :]
        if mixer == "pool":
            prev_ref, meta_ref, mgpre_ref, pw_ref, pb_ref, pscale_ref, mgpost_ref = mixer_refs
            if rows.start == 0:
                halo = jnp.where(pl.program_id(1) == 0, meta_ref[...], prev_ref[...])
            else:
                halo = h_ref[rows.start - HALO:rows.start, :]
            for out in _pool_mix_steps(h, halo, mgpre_ref[...], pw_ref, pb_ref, pscale_ref,
                                       mgpost_ref[...]):
                if out is not None:
                    o_ref[rows, :] = out
                yield
        elif mixer == "oproj":
            attn_ref, wo_ref, mgpost_ref = mixer_refs
            m = jnp.dot(attn_ref[rows, :], wo_ref[...], preferred_element_type=F32)
            o_ref[rows, :] = h + _rmsnorm(m, mgpost_ref[...])
            yield

    def ffn_steps(rows):
        h = h_ref[rows, :] if mixer is None else o_ref[rows, :]
        r = lax.rsqrt(jnp.mean(h * h, axis=-1, keepdims=True) + EPS)
        hg = (h * gpre_ref[...]).astype(BF16)
        for idx, (lo, size) in enumerate(f_chunks):
            a = jnp.dot(hg, w1_ref[:, lo:lo + size], preferred_element_type=F32) * r
            b = jnp.dot(hg, w3_ref[:, lo:lo + size], preferred_element_type=F32) * r
            g = (jax.nn.silu(a) * b).astype(BF16)
            part = jnp.dot(g, w2_ref[lo:lo + size, :], preferred_element_type=F32)
            if idx == 0:
                acc_ref[rows, :] = part
            else:
                acc_ref[rows, :] += part
            yield
        o_ref[rows, :] = h + 0.5 * _rmsnorm(acc_ref[rows, :], gpost_ref[...])
        yield

    for _ in mixer_steps(groups[0]):
        pass
    for rows, nxt in zip(groups, groups[1:] + [None]):
        for _ in itertools.zip_longest(ffn_steps(rows), mixer_steps(nxt) if nxt else ()):
            pass


def _ffn(h, gpre, w1, w3, w2, gpost, *, layer, slot, tm, sub, seq=None, mixer=None, mixer_args=()):
    n, d = h.shape
    f = w1.shape[-1]
    seq = n if seq is None else seq
    tiles = seq // tm
    f_chunk = 512 if mixer == "pool" else 1024
    f_chunks = tuple((lo, min(f_chunk, f - lo)) for lo in range(0, f, f_chunk))
    row = pl.BlockSpec((tm, d), lambda b, i: (b * tiles + i, 0))
    vec = _resident((1, d))

    def weight(rows, cols):
        return pl.BlockSpec((None, None, rows, cols), lambda b, i: (layer, slot, 0, 0),
                            pipeline_mode=pl.Buffered(1))

    if mixer == "pool":
        prev = pl.BlockSpec(
            (HALO, d), lambda b, i: (jnp.maximum(b * (seq // HALO) + i * (tm // HALO) - 1, 0), 0))
        mixer_ops = (h,) + tuple(mixer_args)
        mixer_specs = [prev, _resident((N_META, d)), vec, _resident(mixer_args[2].shape), vec, vec, vec]
    elif mixer == "oproj":
        mixer_ops = tuple(mixer_args)
        mixer_specs = [row, _resident((d, d)), vec]
    else:
        mixer_ops, mixer_specs = (), []
    assert len(mixer_specs) == N_MIXER_REFS[mixer]

    return pl.pallas_call(
        functools.partial(_ffn_kernel, f_chunks=f_chunks, sub=sub, mixer=mixer),
        out_shape=jax.ShapeDtypeStruct((n, d), F32),
        grid=(n // seq, tiles),
        in_specs=mixer_specs + [row, vec, weight(d, f), weight(d, f), weight(f, d), vec],
        out_specs=row,
        scratch_shapes=[pltpu.VMEM((tm, d), F32)],
        compiler_params=_params(("parallel", "parallel")),
        name="ffn" if mixer is None else "ffn_" + mixer,
    )(*mixer_ops, h, gpre, w1, w3, w2, gpost)


def _pool_mix_steps(h, halo, gpre, w_ref, b_ref, scale_ref, gpost):
    rows = h.shape[0]
    is_meta = halo is None
    hn = _rmsnorm(h, gpre)
    halo_n = jnp.zeros((HALO, D_MODEL), F32) if is_meta else _rmsnorm(halo, gpre)
    ext = jnp.concatenate([halo_n, hn], axis=0)
    yield None
    ys = []
    for gi, win in enumerate(POOL_WINDOWS):
        cols = slice(gi * POOL_GROUP, (gi + 1) * POOL_GROUP)
        acc = ext[:, cols]
        step = 1
        while step < win:
            acc = acc + pltpu.roll(acc, step, axis=0)
            step *= 2
        wsum = acc[HALO:]
        if is_meta:
            pos = lax.broadcasted_iota(jnp.int32, (rows, POOL_GROUP), 0)
            pooled = wsum / jnp.minimum(pos + 1, win).astype(F32)
        else:
            pooled = wsum * (1.0 / win)
        diff = (pooled - hn[:, cols]).astype(BF16)
        y = jnp.dot(diff, w_ref[gi], preferred_element_type=F32) + b_ref[:, cols]
        ys.append(y * scale_ref[:, cols])
        yield None
    m = jnp.concatenate(ys, axis=1)
    yield h + _rmsnorm(m, gpost)


def _pool_meta_kernel(h_ref, gpre_ref, w_ref, b_ref, scale_ref, gpost_ref, o_ref):
    *_, o_ref[...] = _pool_mix_steps(h_ref[...], None, gpre_ref[...], w_ref, b_ref, scale_ref, gpost_ref[...])


def _pool_meta(h, gpre, w, b, scale, gpost):
    n, d = h.shape
    vec = _resident((1, d))
    return pl.pallas_call(
        _pool_meta_kernel,
        out_shape=jax.ShapeDtypeStruct((n, d), F32),
        grid=(1,),
        in_specs=[_resident((n, d)), vec, _resident(w.shape), vec, vec, vec],
        out_specs=pl.BlockSpec((n, d), lambda i: (0, 0)),
        compiler_params=_params(("arbitrary",)),
        name="pool_meta",
    )(h, gpre, w, b, scale, gpost)


def _qkv_kernel(h_ref, g_ref, w_ref, q_ref, k_ref, v_ref, *, sub):
    d = D_MODEL
    for r0 in range(0, h_ref.shape[0], sub):
        rows = slice(r0, r0 + sub)
        h = h_ref[rows, :]
        r = lax.rsqrt(jnp.mean(h * h, axis=-1, keepdims=True) + EPS)
        hg = (h * g_ref[...]).astype(BF16)
        q = jnp.dot(hg, w_ref[:, 0:d], preferred_element_type=F32)
        q_ref[rows, :] = (q * (r * (HEAD_DIM ** -0.5 * LOG2_E))).astype(BF16)
        k = jnp.dot(hg, w_ref[:, d:2 * d], preferred_element_type=F32)
        k_ref[rows, :] = (k * r).astype(BF16)
        v = jnp.dot(hg, w_ref[:, 2 * d:3 * d], preferred_element_type=F32)
        v_ref[rows, :] = (v * r).astype(BF16)


def _qkv(h, g, w, *, tm, sub):
    n, d = h.shape
    row = pl.BlockSpec((tm, d), lambda i: (i, 0))
    out = jax.ShapeDtypeStruct((n, d), BF16)
    return pl.pallas_call(
        functools.partial(_qkv_kernel, sub=sub),
        out_shape=(out, out, out),
        grid=(n // tm,),
        in_specs=[row, _resident((1, d)), _resident((d, 3 * d))],
        out_specs=(row, row, row),
        compiler_params=_params(("parallel",)),
        name="qkv",
    )(h, g, w)


def _attn_kernel(q_ref, k_ref, v_ref, km_ref, vm_ref, lq1_ref, lk1_ref, lq2_ref, lk2_ref, subln_ref,
                 o_ref, qq_ref, vx_ref, kmx_ref, vmx_ref, m_ref, acc_ref, *, tq, tk, lambda_init):
    qi = pl.program_id(2)
    nt = (((1,), (1,)), ((), ()))
    hw = HEAD_WIDTH

    @pl.when(qi == 0)
    def _stage_keys_values():
        vx_ref[:, 0:hw] = v_ref[...]
        vx_ref[:, hw:2 * hw] = jnp.ones((v_ref.shape[0], hw), BF16)
        kmx_ref[...] = jnp.zeros(kmx_ref.shape, BF16)
        kmx_ref[0:N_META, :] = km_ref[...]
        vmx_ref[...] = jnp.zeros(vmx_ref.shape, BF16)
        vmx_ref[0:N_META, 0:hw] = vm_ref[...]
        vmx_ref[0:N_META, hw:2 * hw] = jnp.ones((N_META, hw), BF16)

    q = q_ref[...]
    lane = lax.broadcasted_iota(jnp.int32, q.shape, 1)
    zero = jnp.zeros_like(q)
    qq_ref[0:tq, :] = jnp.where(lane < HEAD_DIM, q, zero)
    qq_ref[tq:2 * tq, :] = jnp.where(lane >= HEAD_DIM, q, zero)

    def scores(item):
        r0, nrows, keys_ref, _, k0, n, masks, _ = item
        s = lax.dot_general(qq_ref[r0:r0 + nrows, :], keys_ref[k0:k0 + n, :], nt,
                            preferred_element_type=F32)
        blocks = []
        for j, keep in enumerate(masks):
            blk = s[:, j * LANES:(j + 1) * LANES]
            blocks.append(blk if keep is None else jnp.where(keep, blk, MASK_VALUE))
        m_tile = jnp.max(functools.reduce(jnp.maximum, blocks), axis=1, keepdims=True)
        return blocks, m_tile

    def accumulate(item, blocks, m_tile):
        r0, nrows, _, values_ref, k0, n, _, first = item
        vx = values_ref[k0:k0 + n, :]
        rows = slice(r0, r0 + nrows)
        if first:
            m_new = jnp.broadcast_to(m_tile, (nrows, LANES))
        else:
            m_prev = m_ref[rows, :]
            m_new = jnp.maximum(m_prev, m_tile)
            alpha = jnp.exp2(m_prev - m_new)
        p = jnp.concatenate([jnp.exp2(blk - m_new) for blk in blocks], axis=1).astype(BF16)
        pv = jnp.dot(p, vx, preferred_element_type=F32)
        if first:
            acc_ref[rows, :] = pv
        else:
            acc_ref[rows, :] = jnp.concatenate([alpha, alpha], axis=1) * acc_ref[rows, :] + pv
        m_ref[rows, :] = m_new

    def query_tile(i):
        start = i * tq
        meta_cols = lax.broadcasted_iota(jnp.int32, (tq, LANES), 1)
        items = [(half * tq, tq, kmx_ref, vmx_ref, 0, LANES, [meta_cols < N_META], True)
                 for half in range(2)]
        for lo in range(0, start, 4 * tk):
            for r0 in range(0, 2 * tq, tk):
                items.append((r0, tk, k_ref, vx_ref, lo, 4 * tk, [None] * (4 * tk // LANES), False))
        for lo in range(0, tq, tk):
            nrows = tq - lo
            row_id = lax.broadcasted_iota(jnp.int32, (nrows, LANES), 0)
            col_id = lax.broadcasted_iota(jnp.int32, (nrows, LANES), 1)
            tri = [col_id + j * LANES <= row_id for j in range(tk // LANES)]
            for half in range(2):
                items.append((half * tq + lo, nrows, k_ref, vx_ref, start + lo, tk, tri, False))
        pending = None
        for item in items:
            staged = scores(item)
            if pending is not None:
                accumulate(*pending)
            pending = (item,) + staged
        accumulate(*pending)

    for i in range(k_ref.shape[0] // tq):
        pl.when(qi == i)(functools.partial(query_tile, i))

    lam = (jnp.exp(jnp.sum(lq1_ref[...] * lk1_ref[...], axis=1, keepdims=True))
           - jnp.exp(jnp.sum(lq2_ref[...] * lk2_ref[...], axis=1, keepdims=True)) + lambda_init)
    acc = acc_ref[...]
    o = acc[:, 0:hw] / acc[:, hw:2 * hw]
    o = o[0:tq] - lam * o[tq:2 * tq]
    o_ref[...] = (_rmsnorm(o, subln_ref[...]) * (1.0 - lambda_init)).astype(o_ref.dtype)


def _attention(q, k, v, k_meta, v_meta, lq1, lk1, lq2, lk2, subln, *, seq, tq, tk, lambda_init):
    n, d = q.shape
    nq = seq // tq
    hw = HEAD_WIDTH
    lam_spec = _resident((1, HEAD_DIM))
    return pl.pallas_call(
        functools.partial(_attn_kernel, tq=tq, tk=tk, lambda_init=lambda_init),
        out_shape=jax.ShapeDtypeStruct((n, d), BF16),
        grid=(n // seq, N_HEADS, nq),
        in_specs=[
            pl.BlockSpec((tq, hw), lambda b, h, i: (b * nq + i, h)),
            pl.BlockSpec((seq, hw), lambda b, h, i: (b, h)),
            pl.BlockSpec((seq, hw), lambda b, h, i: (b, h)),
            pl.BlockSpec((N_META, hw), lambda b, h, i: (0, h)),
            pl.BlockSpec((N_META, hw), lambda b, h, i: (0, h)),
            lam_spec, lam_spec, lam_spec, lam_spec, _resident((1, hw)),
        ],
        out_specs=pl.BlockSpec((tq, hw), lambda b, h, i: (b * nq + i, h)),
        scratch_shapes=[
            pltpu.VMEM((2 * tq, hw), BF16),
            pltpu.VMEM((seq, 2 * hw), BF16),
            pltpu.VMEM((LANES, hw), BF16),
            pltpu.VMEM((LANES, 2 * hw), BF16),
            pltpu.VMEM((2 * tq, LANES), F32),
            pltpu.VMEM((2 * tq, 2 * hw), F32),
        ],
        compiler_params=_params(("parallel", "parallel", "arbitrary")),
        name="diff_attn",
    )(q, k, v, k_meta, v_meta, lq1, lk1, lq2, lk2, subln)


def kernel(x, meta_tokens, ffn_norm_pre, ffn_norm_post, ffn_w1, ffn_w3, ffn_w2, mix_norm_pre,
           mix_norm_post, pool_w, pool_b, pool_scale, attn_w_qkv, attn_w_o, attn_lambda_q1,
           attn_lambda_k1, attn_lambda_q2, attn_lambda_k2, attn_subln):
    bsz, seq, d = x.shape
    depth = ffn_w1.shape[0]
    assert d == D_MODEL and meta_tokens.shape == (N_META, d) and depth == 2
    tile, group = ROW_TILE, ROW_GROUP
    assert seq % tile == 0 and tile % group == 0
    row = lambda a: a.reshape(1, -1)

    h = x.reshape(bsz * seq, d)
    hm = meta_tokens.astype(x.dtype)

    w1, w3, w2 = ffn_w1.astype(BF16), ffn_w3.astype(BF16), ffn_w2.astype(BF16)

    def ffn_main(h, i, s, **mixer):
        return _ffn(h, row(ffn_norm_pre[i, s]), w1, w3, w2, row(ffn_norm_post[i, s]), layer=i, slot=s,
                    tm=tile, sub=group, seq=seq, **mixer)

    def ffn_meta(hm, i, s):
        return _ffn(hm, row(ffn_norm_pre[i, s]), w1, w3, w2, row(ffn_norm_post[i, s]), layer=i, slot=s,
                    tm=N_META, sub=N_META)

    h, hm = ffn_main(h, 0, 0), ffn_meta(hm, 0, 0)
    pool_args = (row(mix_norm_pre[0]), pool_w[0].astype(BF16), row(pool_b[0]), row(pool_scale[0]),
                 row(mix_norm_post[0]))
    h = ffn_main(h, 0, 1, mixer="pool", mixer_args=(hm,) + pool_args)
    hm = ffn_meta(_pool_meta(hm, *pool_args), 0, 1)

    h, hm = ffn_main(h, 1, 0), ffn_meta(hm, 1, 0)
    g_pre = row(mix_norm_pre[1])
    w_qkv = attn_w_qkv[0].astype(BF16)
    q, k, v = _qkv(h, g_pre, w_qkv, tm=tile, sub=group)
    _, k_meta, v_meta = _qkv(hm, g_pre, w_qkv, tm=N_META, sub=N_META)
    o = _attention(q, k, v, k_meta, v_meta, row(attn_lambda_q1[0]), row(attn_lambda_k1[0]),
                   row(attn_lambda_q2[0]), row(attn_lambda_k2[0]), row(attn_subln[0]),
                   seq=seq, tq=tile, tk=MXU_TILE, lambda_init=_lambda_init(1))
    h = ffn_main(h, 1, 1, mixer="oproj",
                 mixer_args=(o, attn_w_o[0].astype(BF16), row(mix_norm_post[1])))
    return h.reshape(bsz, seq, d)
```

```python
import functools
import itertools
import math

import jax
import jax.numpy as jnp
from jax import lax
from jax.experimental import pallas as pl
from jax.experimental.pallas import tpu as pltpu

D_MODEL = 1024
N_META = 16
POOL_WINDOWS = (2, 4, 8, 16)
POOL_GROUP = D_MODEL // len(POOL_WINDOWS)
HEAD_DIM = 64
HEAD_WIDTH = 2 * HEAD_DIM
N_HEADS = D_MODEL // HEAD_WIDTH
EPS = 1e-6
HALO = 16
MASK_VALUE = float(jnp.finfo(jnp.float32).min)
LOG2_E = math.log2(math.e)

LANES = 128
MXU_TILE = 256
ROW_TILE = 1024
ROW_GROUP = 512
VMEM_LIMIT_BYTES = 56 * 1024 * 1024

F32 = jnp.float32
BF16 = jnp.bfloat16


def _lambda_init(layer_idx):
    return 0.8 - 0.6 * math.exp(-0.3 * layer_idx)


def _rmsnorm(x, g):
    return x * lax.rsqrt(jnp.mean(x * x, axis=-1, keepdims=True) + EPS) * g


def _resident(shape):
    return pl.BlockSpec(shape, lambda *_: (0,) * len(shape), pipeline_mode=pl.Buffered(1))


def _params(semantics):
    return pltpu.CompilerParams(dimension_semantics=semantics, vmem_limit_bytes=VMEM_LIMIT_BYTES)


N_MIXER_REFS = {None: 0, "pool": 7, "oproj": 3}


def _ffn_kernel(*refs, f_chunks, sub, mixer):
    mixer_refs = refs[:N_MIXER_REFS[mixer]]
    h_ref, gpre_ref, w1_ref, w3_ref, w2_ref, gpost_ref, o_ref, acc_ref = refs[N_MIXER_REFS[mixer]:]
    groups = [slice(r0, r0 + sub) for r0 in range(0, h_ref.shape[0], sub)]

    def mixer_steps(rows):
        h = h_ref[rows, :]
        if mixer == "pool":
            prev_ref, meta_ref, mgpre_ref, pw_ref, pb_ref, pscale_ref, mgpost_ref = mixer_refs
            if rows.start == 0:
                halo = jnp.where(pl.program_id(1) == 0, meta_ref[...], prev_ref[...])
            else:
                halo = h_ref[rows.start - HALO:rows.start, :]
            for out in _pool_mix_steps(h, halo, mgpre_ref[...], pw_ref, pb_ref, pscale_ref,
                                       mgpost_ref[...]):
                if out is not None:
                    o_ref[rows, :] = out
                yield
        elif mixer == "oproj":
            attn_ref, wo_ref, mgpost_ref = mixer_refs
            m = jnp.dot(attn_ref[rows, :], wo_ref[...], preferred_element_type=F32)
            o_ref[rows, :] = h + _rmsnorm(m, mgpost_ref[...])
            yield

    def ffn_steps(rows):
        h = h_ref[rows, :] if mixer is None else o_ref[rows, :]
        r = lax.rsqrt(jnp.mean(h * h, axis=-1, keepdims=True) + EPS)
        hg = (h * gpre_ref[...]).astype(BF16)
        for idx, (lo, size) in enumerate(f_chunks):
            a = jnp.dot(hg, w1_ref[:, lo:lo + size], preferred_element_type=F32) * r
            b = jnp.dot(hg, w3_ref[:, lo:lo + size], preferred_element_type=F32) * r
            g = (jax.nn.silu(a) * b).astype(BF16)
            part = jnp.dot(g, w2_ref[lo:lo + size, :], preferred_element_type=F32)
            if idx == 0:
                acc_ref[rows, :] = part
            else:
                acc_ref[rows, :] += part
            yield
        o_ref[rows, :] = h + 0.5 * _rmsnorm(acc_ref[rows, :], gpost_ref[...])
        yield

    if mixer is None:
        for _ in itertools.zip_longest(*[ffn_steps(rows) for rows in groups]):
            pass
        return
    for _ in mixer_steps(groups[0]):
        pass
    for rows, nxt in zip(groups, groups[1:] + [None]):
        for _ in itertools.zip_longest(ffn_steps(rows), mixer_steps(nxt) if nxt else ()):
            pass


def _ffn(h, gpre, w1, w3, w2, gpost, *, layer, slot, tm, sub, seq=None, mixer=None, mixer_args=()):
    n, d = h.shape
    f = w1.shape[-1]
    seq = n if seq is None else seq
    tiles = seq // tm
    f_chunk = 512 if mixer == "pool" else 1024
    f_chunks = tuple((lo, min(f_chunk, f - lo)) for lo in range(0, f, f_chunk))
    row = pl.BlockSpec((tm, d), lambda b, i: (b * tiles + i, 0))
    vec = _resident((1, d))

    def weight(rows, cols):
        return pl.BlockSpec((None, None, rows, cols), lambda b, i: (layer, slot, 0, 0),
                            pipeline_mode=pl.Buffered(1))

    if mixer == "pool":
        prev = pl.BlockSpec(
            (HALO, d), lambda b, i: (jnp.maximum(b * (seq // HALO) + i * (tm // HALO) - 1, 0), 0))
        mixer_ops = (h,) + tuple(mixer_args)
        mixer_specs = [prev, _resident((N_META, d)), vec, _resident(mixer_args[2].shape), vec, vec, vec]
    elif mixer == "oproj":
        mixer_ops = tuple(mixer_args)
        mixer_specs = [row, _resident((d, d)), vec]
    else:
        mixer_ops, mixer_specs = (), []
    assert len(mixer_specs) == N_MIXER_REFS[mixer]

    return pl.pallas_call(
        functools.partial(_ffn_kernel, f_chunks=f_chunks, sub=sub, mixer=mixer),
        out_shape=jax.ShapeDtypeStruct((n, d), F32),
        grid=(n // seq, tiles),
        in_specs=mixer_specs + [row, vec, weight(d, f), weight(d, f), weight(f, d), vec],
        out_specs=row,
        scratch_shapes=[pltpu.VMEM((tm, d), F32)],
        compiler_params=_params(("parallel", "parallel")),
        name="ffn" if mixer is None else "ffn_" + mixer,
    )(*mixer_ops, h, gpre, w1, w3, w2, gpost)


def _pool_mix_steps(h, halo, gpre, w_ref, b_ref, scale_ref, gpost):
    rows = h.shape[0]
    is_meta = halo is None
    hn = _rmsnorm(h, gpre)
    halo_n = jnp.zeros((HALO, D_MODEL), F32) if is_meta else _rmsnorm(halo, gpre)
    ext = jnp.concatenate([halo_n, hn], axis=0)
    yield None
    ys = []
    for gi, win in enumerate(POOL_WINDOWS):
        cols = slice(gi * POOL_GROUP, (gi + 1) * POOL_GROUP)
        acc = ext[:, cols]
        step = 1
        while step < win:
            acc = acc + pltpu.roll(acc, step, axis=0)
            step *= 2
        wsum = acc[HALO:]
        if is_meta:
            pos = lax.broadcasted_iota(jnp.int32, (rows, POOL_GROUP), 0)
            pooled = wsum / jnp.minimum(pos + 1, win).astype(F32)
        else:
            pooled = wsum * (1.0 / win)
        diff = (pooled - hn[:, cols]).astype(BF16)
        y = jnp.dot(diff, w_ref[gi], preferred_element_type=F32) + b_ref[:, cols]
        ys.append(y * scale_ref[:, cols])
        yield None
    m = jnp.concatenate(ys, axis=1)
    yield h + _rmsnorm(m, gpost)


def _pool_meta_kernel(h_ref, gpre_ref, w_ref, b_ref, scale_ref, gpost_ref, o_ref):
    *_, o_ref[...] = _pool_mix_steps(h_ref[...], None, gpre_ref[...], w_ref, b_ref, scale_ref, gpost_ref[...])


def _pool_meta(h, gpre, w, b, scale, gpost):
    n, d = h.shape
    vec = _resident((1, d))
    return pl.pallas_call(
        _pool_meta_kernel,
        out_shape=jax.ShapeDtypeStruct((n, d), F32),
        grid=(1,),
        in_specs=[_resident((n, d)), vec, _resident(w.shape), vec, vec, vec],
        out_specs=pl.BlockSpec((n, d), lambda i: (0, 0)),
        compiler_params=_params(("arbitrary",)),
        name="pool_meta",
    )(h, gpre, w, b, scale, gpost)


def _qkv_kernel(h_ref, g_ref, w_ref, q_ref, k_ref, v_ref, *, sub):
    d = D_MODEL
    for r0 in range(0, h_ref.shape[0], sub):
        rows = slice(r0, r0 + sub)
        h = h_ref[rows, :]
        r = lax.rsqrt(jnp.mean(h * h, axis=-1, keepdims=True) + EPS)
        hg = (h * g_ref[...]).astype(BF16)
        q = jnp.dot(hg, w_ref[:, 0:d], preferred_element_type=F32)
        q_ref[rows, :] = (q * (r * (HEAD_DIM ** -0.5 * LOG2_E))).astype(BF16)
        k = jnp.dot(hg, w_ref[:, d:2 * d], preferred_element_type=F32)
        k_ref[rows, :] = (k * r).astype(BF16)
        v = jnp.dot(hg, w_ref[:, 2 * d:3 * d], preferred_element_type=F32)
        v_ref[rows, :] = (v * r).astype(BF16)


def _qkv(h, g, w, *, tm, sub):
    n, d = h.shape
    row = pl.BlockSpec((tm, d), lambda i: (i, 0))
    out = jax.ShapeDtypeStruct((n, d), BF16)
    return pl.pallas_call(
        functools.partial(_qkv_kernel, sub=sub),
        out_shape=(out, out, out),
        grid=(n // tm,),
        in_specs=[row, _resident((1, d)), _resident((d, 3 * d))],
        out_specs=(row, row, row),
        compiler_params=_params(("parallel",)),
        name="qkv",
    )(h, g, w)


def _attn_kernel(q_ref, k_ref, v_ref, km_ref, vm_ref, lq1_ref, lk1_ref, lq2_ref, lk2_ref, subln_ref,
                 o_ref, qq_ref, vx_ref, kmx_ref, vmx_ref, m_ref, acc_ref, *, tq, tk, lambda_init):
    qi = pl.program_id(2)
    nt = (((1,), (1,)), ((), ()))
    hw = HEAD_WIDTH

    @pl.when(qi == 0)
    def _stage_keys_values():
        vx_ref[:, 0:hw] = v_ref[...]
        vx_ref[:, hw:2 * hw] = jnp.ones((v_ref.shape[0], hw), BF16)
        kmx_ref[...] = jnp.zeros(kmx_ref.shape, BF16)
        kmx_ref[0:N_META, :] = km_ref[...]
        vmx_ref[...] = jnp.zeros(vmx_ref.shape, BF16)
        vmx_ref[0:N_META, 0:hw] = vm_ref[...]
        vmx_ref[0:N_META, hw:2 * hw] = jnp.ones((N_META, hw), BF16)

    q = q_ref[...]
    lane = lax.broadcasted_iota(jnp.int32, q.shape, 1)
    zero = jnp.zeros_like(q)
    qq_ref[0:tq, :] = jnp.where(lane < HEAD_DIM, q, zero)
    qq_ref[tq:2 * tq, :] = jnp.where(lane >= HEAD_DIM, q, zero)

    def scores(item):
        r0, nrows, keys_ref, _, k0, n, masks, _ = item
        s = lax.dot_general(qq_ref[r0:r0 + nrows, :], keys_ref[k0:k0 + n, :], nt,
                            preferred_element_type=F32)
        blocks = []
        for j, keep in enumerate(masks):
            blk = s[:, j * LANES:(j + 1) * LANES]
            blocks.append(blk if keep is None else jnp.where(keep, blk, MASK_VALUE))
        m_tile = jnp.max(functools.reduce(jnp.maximum, blocks), axis=1, keepdims=True)
        return blocks, m_tile

    def accumulate(item, blocks, m_tile):
        r0, nrows, _, values_ref, k0, n, _, first = item
        vx = values_ref[k0:k0 + n, :]
        rows = slice(r0, r0 + nrows)
        if first:
            m_new = jnp.broadcast_to(m_tile, (nrows, LANES))
        else:
            m_prev = m_ref[rows, :]
            m_new = jnp.maximum(m_prev, m_tile)
            alpha = jnp.exp2(m_prev - m_new)
        p = jnp.concatenate([jnp.exp2(blk - m_new) for blk in blocks], axis=1).astype(BF16)
        pv = jnp.dot(p, vx, preferred_element_type=F32)
        if first:
            acc_ref[rows, :] = pv
        else:
            acc_ref[rows, :] = jnp.concatenate([alpha, alpha], axis=1) * acc_ref[rows, :] + pv
        m_ref[rows, :] = m_new

    def query_tile(i):
        start = i * tq
        meta_cols = lax.broadcasted_iota(jnp.int32, (tq, LANES), 1)
        items = [(half * tq, tq, kmx_ref, vmx_ref, 0, LANES, [meta_cols < N_META], True)
                 for half in range(2)]
        for lo in range(0, start, 4 * tk):
            for r0 in range(0, 2 * tq, tk):
                items.append((r0, tk, k_ref, vx_ref, lo, 4 * tk, [None] * (4 * tk // LANES), False))
        for lo in range(0, tq, tk):
            nrows = tq - lo
            row_id = lax.broadcasted_iota(jnp.int32, (nrows, LANES), 0)
            col_id = lax.broadcasted_iota(jnp.int32, (nrows, LANES), 1)
            tri = [col_id + j * LANES <= row_id for j in range(tk // LANES)]
            for half in range(2):
                items.append((half * tq + lo, nrows, k_ref, vx_ref, start + lo, tk, tri, False))
        pending = None
        for item in items:
            staged = scores(item)
            if pending is not None:
                accumulate(*pending)
            pending = (item,) + staged
        accumulate(*pending)

    for i in range(k_ref.shape[0] // tq):
        pl.when(qi == i)(functools.partial(query_tile, i))

    lam = (jnp.exp(jnp.sum(lq1_ref[...] * lk1_ref[...], axis=1, keepdims=True))
           - jnp.exp(jnp.sum(lq2_ref[...] * lk2_ref[...], axis=1, keepdims=True)) + lambda_init)
    acc = acc_ref[...]
    o = acc[:, 0:hw] / acc[:, hw:2 * hw]
    o = o[0:tq] - lam * o[tq:2 * tq]
    o_ref[...] = (_rmsnorm(o, subln_ref[...]) * (1.0 - lambda_init)).astype(o_ref.dtype)


def _attention(q, k, v, k_meta, v_meta, lq1, lk1, lq2, lk2, subln, *, seq, tq, tk, lambda_init):
    n, d = q.shape
    nq = seq // tq
    hw = HEAD_WIDTH
    lam_spec = _resident((1, HEAD_DIM))
    return pl.pallas_call(
        functools.partial(_attn_kernel, tq=tq, tk=tk, lambda_init=lambda_init),
        out_shape=jax.ShapeDtypeStruct((n, d), BF16),
        grid=(n // seq, N_HEADS, nq),
        in_specs=[
            pl.BlockSpec((tq, hw), lambda b, h, i: (b * nq + i, h)),
            pl.BlockSpec((seq, hw), lambda b, h, i: (b, h)),
            pl.BlockSpec((seq, hw), lambda b, h, i: (b, h)),
            pl.BlockSpec((N_META, hw), lambda b, h, i: (0, h)),
            pl.BlockSpec((N_META, hw), lambda b, h, i: (0, h)),
            lam_spec, lam_spec, lam_spec, lam_spec, _resident((1, hw)),
        ],
        out_specs=pl.BlockSpec((tq, hw), lambda b, h, i: (b * nq + i, h)),
        scratch_shapes=[
            pltpu.VMEM((2 * tq, hw), BF16),
            pltpu.VMEM((seq, 2 * hw), BF16),
            pltpu.VMEM((LANES, hw), BF16),
            pltpu.VMEM((LANES, 2 * hw), BF16),
            pltpu.VMEM((2 * tq, LANES), F32),
            pltpu.VMEM((2 * tq, 2 * hw), F32),
        ],
        compiler_params=_params(("parallel", "parallel", "arbitrary")),
        name="diff_attn",
    )(q, k, v, k_meta, v_meta, lq1, lk1, lq2, lk2, subln)


def kernel(x, meta_tokens, ffn_norm_pre, ffn_norm_post, ffn_w1, ffn_w3, ffn_w2, mix_norm_pre,
           mix_norm_post, pool_w, pool_b, pool_scale, attn_w_qkv, attn_w_o, attn_lambda_q1,
           attn_lambda_k1, attn_lambda_q2, attn_lambda_k2, attn_subln):
    bsz, seq, d = x.shape
    depth = ffn_w1.shape[0]
    assert d == D_MODEL and meta_tokens.shape == (N_META, d) and depth == 2
    tile, group = ROW_TILE, ROW_GROUP
    assert seq % tile == 0 and tile % group == 0
    row = lambda a: a.reshape(1, -1)

    h = x.reshape(bsz * seq, d)
    hm = meta_tokens.astype(x.dtype)

    w1, w3, w2 = ffn_w1.astype(BF16), ffn_w3.astype(BF16), ffn_w2.astype(BF16)

    def ffn_main(h, i, s, **mixer):
        return _ffn(h, row(ffn_norm_pre[i, s]), w1, w3, w2, row(ffn_norm_post[i, s]), layer=i, slot=s,
                    tm=tile, sub=group, seq=seq, **mixer)

    def ffn_meta(hm, i, s):
        return _ffn(hm, row(ffn_norm_pre[i, s]), w1, w3, w2, row(ffn_norm_post[i, s]), layer=i, slot=s,
                    tm=N_META, sub=N_META)

    h, hm = ffn_main(h, 0, 0), ffn_meta(hm, 0, 0)
    pool_args = (row(mix_norm_pre[0]), pool_w[0].astype(BF16), row(pool_b[0]), row(pool_scale[0]),
                 row(mix_norm_post[0]))
    h = ffn_main(h, 0, 1, mixer="pool", mixer_args=(hm,) + pool_args)
    hm = ffn_meta(_pool_meta(hm, *pool_args), 0, 1)

    h, hm = ffn_main(h, 1, 0), ffn_meta(hm, 1, 0)
    g_pre = row(mix_norm_pre[1])
    w_qkv = attn_w_qkv[0].astype(BF16)
    q, k, v = _qkv(h, g_pre, w_qkv, tm=tile, sub=group)
    _, k_meta, v_meta = _qkv(hm, g_pre, w_qkv, tm=N_META, sub=N_META)
    o = _attention(q, k, v, k_meta, v_meta, row(attn_lambda_q1[0]), row(attn_lambda_k1[0]),
                   row(attn_lambda_q2[0]), row(attn_lambda_k2[0]), row(attn_subln[0]),
                   seq=seq, tq=tile, tk=MXU_TILE, lambda_init=_lambda_init(1))
    h = ffn_main(h, 1, 1, mixer="oproj",
                 mixer_args=(o, attn_w_o[0].astype(BF16), row(mix_norm_post[1])))
    return h.reshape(bsz, seq, d)
```
